```python
import math
import numpy as np
import jax
import jax.numpy as jnp
from jax import lax

D_MODEL = 1024
BATCH = 8
SEQ = 4096
DEPTH = 1

HEAD_DIM = 64
GRID_W = 64
NA_HEADS = 8
NA_WIN_ROWS = 8
NA_WIN_COLS = 16
NA_QCOL_BLOCK = 16
NA_KCOL_BLOCK = NA_QCOL_BLOCK + NA_WIN_COLS
DIL_CONFIGS = ((128, 1), (512, 4), (2048, 16))
DIL_HEADS_PER_GROUP = 4
DIL_N_GROUPS = len(DIL_CONFIGS)
DIL_QBLOCK = 64
ROT_DIM = HEAD_DIM // 4
ROPE_THETA = 500000.0
NA_WIDTH = NA_HEADS * HEAD_DIM
DIL_WIDTH = DIL_N_GROUPS * DIL_HEADS_PER_GROUP * HEAD_DIM
DIL_OUT_WIDTH = DIL_HEADS_PER_GROUP * HEAD_DIM
IN_WIDTH = 3 * NA_WIDTH + 3 * DIL_WIDTH + 2 * D_MODEL
PEER_HEADS = 8
PEER_NKEYS = 128
PEER_NEXPERTS = PEER_NKEYS * PEER_NKEYS
PEER_QDIM = 256
PEER_TOPK = 16
PEER_CHUNK = 256
PLE_DIM = 256
RMS_EPS = 1e-6

kernel_name = "hybrid_na_dilated_peer_block"


def rms_norm(x, gain):
    x32 = x.astype(jnp.float32)
    y = x32 * lax.rsqrt(jnp.mean(x32 * x32, axis=-1, keepdims=True) + RMS_EPS)
    return (y * gain.astype(jnp.float32)).astype(x.dtype)


def partial_rotary(x, pos):
    half = ROT_DIM // 2
    inv_freq = jnp.power(jnp.float32(ROPE_THETA), -jnp.arange(half, dtype=jnp.float32) * 2.0 / ROT_DIM)
    ang = pos.astype(jnp.float32)[:, None] * inv_freq[None, :]
    cos = jnp.cos(ang)[:, None, :]
    sin = jnp.sin(ang)[:, None, :]
    xr = x[..., :ROT_DIM].astype(jnp.float32)
    x1, x2 = xr[..., :half], xr[..., half:]
    rot = jnp.concatenate([x1 * cos - x2 * sin, x2 * cos + x1 * sin], axis=-1)
    return jnp.concatenate([rot.astype(x.dtype), x[..., ROT_DIM:]], axis=-1)


def _na_static(width):
    ncb = width // NA_QCOL_BLOCK
    qc = np.arange(width).reshape(ncb, NA_QCOL_BLOCK)
    kc_start = np.clip(np.arange(ncb) * NA_QCOL_BLOCK - NA_WIN_COLS // 2, 0, width - NA_KCOL_BLOCK)
    kc = kc_start[:, None] + np.arange(NA_KCOL_BLOCK)[None, :]
    cs = np.clip(qc - NA_WIN_COLS // 2, 0, width - NA_WIN_COLS)
    kcb = kc[:, None, :]
    mask = (kcb >= cs[..., None]) & (kcb < cs[..., None] + NA_WIN_COLS)
    dc_idx = np.clip(kcb - qc[..., None] + NA_WIN_COLS - 1, 0, 2 * NA_WIN_COLS - 2)
    return kc, mask, dc_idx


def neighbourhood_attention(q, k, v, rpb):
    B, S, H, dh = q.shape
    rows = S // GRID_W
    kh = min(NA_WIN_ROWS, rows)
    ncb = GRID_W // NA_QCOL_BLOCK
    kc, mask, dc_idx = _na_static(GRID_W)
    mask_j = jnp.asarray(mask)[None, None, :, :, None, :]
    qg = q.reshape(B, rows, GRID_W, H, dh)
    kg = k.reshape(B, rows, GRID_W, H, dh)
    vg = v.reshape(B, rows, GRID_W, H, dh)
    scale = HEAD_DIM ** -0.5

    def one_row(r):
        rs = jnp.clip(r - kh // 2, 0, rows - kh)
        q_row = lax.dynamic_index_in_dim(qg, r, axis=1, keepdims=False)
        k_rows = lax.dynamic_slice_in_dim(kg, rs, kh, axis=1)
        v_rows = lax.dynamic_slice_in_dim(vg, rs, kh, axis=1)
        q_cb = q_row.reshape(B, ncb, NA_QCOL_BLOCK, H, dh)
        k_cb = jnp.take(k_rows, kc, axis=2)
        v_cb = jnp.take(v_rows, kc, axis=2)
        s = jnp.einsum('bjqhd,bljkhd->bhjqlk', q_cb, k_cb).astype(jnp.float32) * scale
        dr_idx = rs + jnp.arange(kh) - r + NA_WIN_ROWS - 1
        bias = rpb[:, dr_idx][:, :, dc_idx]
        bias = bias.transpose(0, 2, 3, 1, 4).astype(jnp.float32)
        s = jnp.where(mask_j, s + bias[None], -jnp.inf)
        shp = s.shape
        pr = jax.nn.softmax(s.reshape(shp[:4] + (kh * NA_KCOL_BLOCK,)), axis=-1).reshape(shp)
        o = jnp.einsum('bhjqlk,bljkhd->bjqhd', pr.astype(v.dtype), v_cb)
        return o.reshape(B, GRID_W, H, dh)

    out = lax.map(one_row, jnp.arange(rows))
    return out.transpose(1, 0, 2, 3, 4).reshape(B, S, H * dh)


def dilated_window_attention(q, k, v, window, dilation):
    B, S, H, dh = q.shape
    n = window // (2 * dilation)
    L = S // dilation
    qbs = math.gcd(L, DIL_QBLOCK)
    nblk = L // qbs
    kbs = qbs + 2 * n

    def to_res(t):
        return t.reshape(B, L, dilation, H, dh).transpose(0, 2, 1, 3, 4)

    qr, kr, vr = to_res(q), to_res(k), to_res(v)
    pad = ((0, 0), (0, 0), (n, n), (0, 0), (0, 0))
    kp = jnp.pad(kr, pad)
    vp = jnp.pad(vr, pad)
    kidx = (np.arange(nblk) * qbs)[:, None] + np.arange(kbs)[None, :]
    kb = jnp.take(kp, kidx, axis=2)
    vb = jnp.take(vp, kidx, axis=2)
    qb = qr.reshape(B, dilation, nblk, qbs, H, dh)
    s = jnp.einsum('brnqhd,brnkhd->brnhqk', qb, kb).astype(jnp.float32) * (HEAD_DIM ** -0.5)
    qi = np.arange(qbs)[:, None]
    kj = np.arange(kbs)[None, :]
    band = (kj - qi >= 0) & (kj - qi <= 2 * n)
    m_key = kidx - n
    inside = (m_key >= 0) & (m_key < L)
    mask = band[None] & inside[:, None, :]
    s = jnp.where(jnp.asarray(mask)[None, None, :, None, :, :], s, -jnp.inf)
    lse = jax.nn.logsumexp(s, axis=-1, keepdims=True)
    pr = jnp.exp(s - lse).astype(v.dtype)
    o = jnp.einsum('brnhqk,brnkhd->brnqhd', pr, vb)
    o = o.reshape(B, dilation, L, H, dh).transpose(0, 2, 1, 3, 4).reshape(B, S, H, dh)
    lse = lse[..., 0].transpose(0, 1, 2, 4, 3).reshape(B, dilation, L, H)
    lse = lse.transpose(0, 2, 1, 3).reshape(B, S, H)
    return o, lse


def peer_ffn(h, w_query, sub_keys, expert_u, expert_v):
    B, S, D = h.shape
    T = B * S
    chunk = math.gcd(T, PEER_CHUNK)
    half = PEER_QDIM // 2
    hf = h.reshape(T // chunk, chunk, D)
    k1 = sub_keys[0].astype(jnp.float32)
    k2 = sub_keys[1].astype(jnp.float32)

    def run(hc):
        q = (hc @ w_query).reshape(chunk, PEER_HEADS, PEER_QDIM).astype(jnp.float32)
        s1 = jnp.einsum('thd,nd->thn', q[..., :half], k1)
        s2 = jnp.einsum('thd,nd->thn', q[..., half:], k2)
        v1, i1 = lax.top_k(s1, PEER_TOPK)
        v2, i2 = lax.top_k(s2, PEER_TOPK)
        cand = (v1[..., :, None] + v2[..., None, :]).reshape(chunk, PEER_HEADS, PEER_TOPK * PEER_TOPK)
        cidx = (i1[..., :, None] * PEER_NKEYS + i2[..., None, :]).reshape(chunk, PEER_HEADS, PEER_TOPK * PEER_TOPK)
        sc, pos = lax.top_k(cand, PEER_TOPK)
        eidx = jnp.take_along_axis(cidx, pos, axis=-1)
        g = jax.nn.softmax(sc, axis=-1)
        u = jnp.take(expert_u, eidx, axis=0)
        a = jnp.einsum('thkd,td->thk', u, hc).astype(jnp.float32)
        act = (jax.nn.gelu(a, approximate=False) * g).astype(hc.dtype)
        vv = jnp.take(expert_v, eidx, axis=0)
        return jnp.einsum('thk,thkd->td', act, vv)

    return lax.map(run, hf).reshape(B, S, D)


def setup_inputs(seed: int = 0) -> dict:
    key = jax.random.key(seed)
    ks = jax.random.split(key, 20)

    def nrm(k, shape, scale):
        return jax.random.normal(k, shape, jnp.float32) * scale

    def gain(k, shape):
        return 1.0 + 0.05 * jax.random.normal(k, shape, jnp.float32)

    return {
        "x": nrm(ks[0], (BATCH, SEQ, D_MODEL), 1.0),
        "p": nrm(ks[1], (DEPTH, BATCH, SEQ, PLE_DIM), 1.0),
        "norm_mix": gain(ks[2], (DEPTH, D_MODEL)),
        "w_in": nrm(ks[3], (DEPTH, D_MODEL, IN_WIDTH), D_MODEL ** -0.5),
        "qk_norm_na": gain(ks[4], (DEPTH, 2, HEAD_DIM)),
        "na_rel_bias": nrm(ks[5], (DEPTH, NA_HEADS, 2 * NA_WIN_ROWS - 1, 2 * NA_WIN_COLS - 1), 0.2),
        "qk_norm_dil": gain(ks[6], (DEPTH, 2, HEAD_DIM)),
        "w_branch_na": nrm(ks[7], (DEPTH, NA_WIDTH, D_MODEL), NA_WIDTH ** -0.5),
        "w_branch_dil": nrm(ks[8], (DEPTH, DIL_OUT_WIDTH, D_MODEL), DIL_OUT_WIDTH ** -0.5),
        "w_out": nrm(ks[9], (DEPTH, D_MODEL, D_MODEL), D_MODEL ** -0.5),
        "norm_ffn": gain(ks[10], (DEPTH, D_MODEL)),
        "peer_w_query": nrm(ks[11], (DEPTH, D_MODEL, PEER_HEADS * PEER_QDIM), D_MODEL ** -0.5),
        "peer_sub_keys": nrm(ks[12], (DEPTH, 2, PEER_NKEYS, PEER_QDIM // 2), (PEER_QDIM // 2) ** -0.5),
        "peer_expert_u": nrm(ks[13], (DEPTH, PEER_NEXPERTS, D_MODEL), D_MODEL ** -0.5),
        "peer_expert_v": nrm(ks[14], (DEPTH, PEER_NEXPERTS, D_MODEL), PEER_HEADS ** -0.5),
        "norm_ple": gain(ks[15], (DEPTH, D_MODEL)),
        "w_ple_gate": nrm(ks[16], (DEPTH, D_MODEL, D_MODEL), D_MODEL ** -0.5),
        "w_ple": nrm(ks[17], (DEPTH, PLE_DIM, D_MODEL), PLE_DIM ** -0.5),
    }


def reference(x, p, norm_mix, w_in, qk_norm_na, na_rel_bias, qk_norm_dil, w_branch_na, w_branch_dil,
              w_out, norm_ffn, peer_w_query, peer_sub_keys, peer_expert_u, peer_expert_v,
              norm_ple, w_ple_gate, w_ple):
    B, S, D = x.shape
    pos = jnp.arange(S)
    splits = [int(c) for c in np.cumsum([NA_WIDTH] * 3 + [DIL_WIDTH] * 3 + [D_MODEL])]
    n_dil_heads = DIL_N_GROUPS * DIL_HEADS_PER_GROUP
    for i in range(DEPTH):
        h = rms_norm(x, norm_mix[i])
        proj = h @ w_in[i]
        qa, ka, va, qd, kd, vd, gate_na, gate_dil = jnp.split(proj, splits, axis=-1)

        qa = rms_norm(qa.reshape(B, S, NA_HEADS, HEAD_DIM), qk_norm_na[i, 0])
        ka = rms_norm(ka.reshape(B, S, NA_HEADS, HEAD_DIM), qk_norm_na[i, 1])
        va = va.reshape(B, S, NA_HEADS, HEAD_DIM)
        out_na = neighbourhood_attention(qa, ka, va, na_rel_bias[i])

        qd = partial_rotary(rms_norm(qd.reshape(B, S, n_dil_heads, HEAD_DIM), qk_norm_dil[i, 0]), pos)
        kd = partial_rotary(rms_norm(kd.reshape(B, S, n_dil_heads, HEAD_DIM), qk_norm_dil[i, 1]), pos)
        qd = qd.reshape(B, S, DIL_N_GROUPS, DIL_HEADS_PER_GROUP, HEAD_DIM)
        kd = kd.reshape(B, S, DIL_N_GROUPS, DIL_HEADS_PER_GROUP, HEAD_DIM)
        vd = vd.reshape(B, S, DIL_N_GROUPS, DIL_HEADS_PER_GROUP, HEAD_DIM)
        outs, lses = [], []
        for g, (window, dilation) in enumerate(DIL_CONFIGS):
            o_g, lse_g = dilated_window_attention(qd[:, :, g], kd[:, :, g], vd[:, :, g], window, dilation)
            outs.append(o_g)
            lses.append(lse_g)
        wts = jax.nn.softmax(jnp.stack(lses, axis=2), axis=2)
        out_dil = jnp.sum(wts[..., None] * jnp.stack(outs, axis=2).astype(jnp.float32), axis=2)
        out_dil = out_dil.astype(x.dtype).reshape(B, S, DIL_OUT_WIDTH)

        merged = (jax.nn.sigmoid(gate_na) * (out_na @ w_branch_na[i])
                  + jax.nn.sigmoid(gate_dil) * (out_dil @ w_branch_dil[i]))
        x = x + merged @ w_out[i]

        h = rms_norm(x, norm_ffn[i])
        x = x + peer_ffn(h, peer_w_query[i], peer_sub_keys[i], peer_expert_u[i], peer_expert_v[i])

        h = rms_norm(x, norm_ple[i])
        x = x + jax.nn.sigmoid(h @ w_ple_gate[i]) * (p[i] @ w_ple[i])
    return x
```

```python
import functools

import numpy as np
import jax
import jax.numpy as jnp
from jax import lax
from jax.experimental import pallas as pl
from jax.experimental.pallas import tpu as pltpu

F32 = jnp.float32
BF16 = jnp.bfloat16

HEAD_DIM = 64
GRID_W = 64
NA_HEADS = 8
NA_WIN_ROWS = 8
NA_WIN_COLS = 16
DIL_CONFIGS = ((128, 1), (512, 4), (2048, 16))
DIL_HEADS_PER_GROUP = 4
DIL_GROUP_WIDTH = DIL_HEADS_PER_GROUP * HEAD_DIM
ROT_DIM = HEAD_DIM // 4
ROPE_THETA = 500000.0
PEER_HEADS = 8
PEER_NKEYS = 128
PEER_QDIM = 256
PEER_TOPK = 16
PEER_SLOTS = PEER_HEADS * PEER_TOPK
RMS_EPS = 1e-6
NEG_BIG = -1e30

LANES = 128
SUBLANES = 8
VMEM_LIMIT_BYTES = 48 * 1024 * 1024


def _cparams(n_axes):
    return pltpu.CompilerParams(dimension_semantics=("arbitrary",) * n_axes, vmem_limit_bytes=VMEM_LIMIT_BYTES)


def _const_spec(shape):
    nd = len(shape)
    return pl.BlockSpec(shape, lambda *_: (0,) * nd, pipeline_mode=pl.Buffered(1))


def _rms(x, gain):
    return x * lax.rsqrt(jnp.mean(x * x, axis=-1, keepdims=True) + RMS_EPS) * gain


def _dot(a, b):
    return jnp.dot(a, b, preferred_element_type=F32)


def _dot_nt(a, b):
    return lax.dot_general(a, b, (((1,), (1,)), ((), ())), preferred_element_type=F32)


def _head_rms(q, bd_ref, gain):
    outs = []
    for c in range(q.shape[1] // 256):
        qc = q[:, c * 256:(c + 1) * 256]
        sq = qc * qc
        hi = sq.astype(BF16)
        lo = (sq - hi.astype(F32)).astype(BF16)
        ssq = _dot(hi, bd_ref[...]) + _dot(lo, bd_ref[...])
        outs.append(qc * lax.rsqrt(ssq * (1.0 / HEAD_DIM) + RMS_EPS))
    return jnp.concatenate(outs, axis=1) * gain


def _rotary(q, ra, rb):
    lane = lax.broadcasted_iota(jnp.int32, q.shape, 1) % HEAD_DIM
    partner = jnp.where(lane < ROT_DIM // 2, pltpu.roll(q, 256 - ROT_DIM // 2, 1), pltpu.roll(q, ROT_DIM // 2, 1))
    return q * ra + partner * rb


def _in_proj_kernel(x_ref, g_ref, wna_ref, wdil_ref, wgate_ref, nna_ref, ndil_ref, ra_ref, rb_ref, bd_ref,
                    qa_ref, ka_ref, va_ref,
                    q1_ref, k1_ref, v1_ref, q2_ref, k2_ref, v2_ref, q3_ref, k3_ref, v3_ref,
                    sgn_ref, sgd_ref):
    h = _rms(x_ref[...], g_ref[...]).astype(BF16)
    scale = HEAD_DIM ** -0.5

    na = _dot(h, wna_ref[...])
    wna = NA_HEADS * HEAD_DIM
    qa_ref[...] = (_head_rms(na[:, :wna], bd_ref, nna_ref[0:1, :]) * scale).astype(BF16)
    ka_ref[...] = _head_rms(na[:, wna:2 * wna], bd_ref, nna_ref[1:2, :]).astype(BF16)
    va_ref[...] = na[:, 2 * wna:].astype(BF16)

    dil = _dot(h, wdil_ref[...])
    wd = len(DIL_CONFIGS) * DIL_GROUP_WIDTH
    ra = ra_ref[...]
    rb = rb_ref[...]
    q_refs = (q1_ref, q2_ref, q3_ref)
    k_refs = (k1_ref, k2_ref, k3_ref)
    v_refs = (v1_ref, v2_ref, v3_ref)
    for g in range(len(DIL_CONFIGS)):
        lo = g * DIL_GROUP_WIDTH
        hi = lo + DIL_GROUP_WIDTH
        q = _head_rms(dil[:, lo:hi], bd_ref, ndil_ref[0:1, :])
        k = _head_rms(dil[:, wd + lo:wd + hi], bd_ref, ndil_ref[1:2, :])
        q_refs[g][...] = (_rotary(q, ra, rb) * scale).astype(BF16)
        k_refs[g][...] = _rotary(k, ra, rb).astype(BF16)
        v_refs[g][...] = dil[:, 2 * wd + lo:2 * wd + hi].astype(BF16)

    gate = _dot(h, wgate_ref[...])
    d = sgn_ref.shape[1]
    sgn_ref[...] = jax.nn.sigmoid(gate[:, :d]).astype(BF16)
    sgd_ref[...] = jax.nn.sigmoid(gate[:, d:]).astype(BF16)


def _in_proj(x2d, gain, w_na, w_dil, w_gate, nna, ndil, rot_a, rot_b, seq, tm=256):
    T, D = x2d.shape
    bd = jnp.asarray(np.kron(np.eye(256 // HEAD_DIM), np.ones((HEAD_DIM, HEAD_DIM))), BF16)
    wna = NA_HEADS * HEAD_DIM
    nseq = seq // tm
    row = lambda w: pl.BlockSpec((tm, w), lambda i: (i, 0))
    rot = pl.BlockSpec((tm, DIL_GROUP_WIDTH), lambda i: (i % nseq, 0))
    outs = [jax.ShapeDtypeStruct((T, wna), BF16)] * 3 + [jax.ShapeDtypeStruct((T, DIL_GROUP_WIDTH), BF16)] * 9 \
        + [jax.ShapeDtypeStruct((T, D), BF16)] * 2
    return pl.pallas_call(
        _in_proj_kernel,
        grid=(T // tm,),
        in_specs=[row(D), _const_spec((1, D)), _const_spec(w_na.shape), _const_spec(w_dil.shape),
                  _const_spec(w_gate.shape), _const_spec(nna.shape), _const_spec(ndil.shape), rot, rot,
                  _const_spec(bd.shape)],
        out_specs=[row(wna)] * 3 + [row(DIL_GROUP_WIDTH)] * 9 + [row(D)] * 2,
        out_shape=outs,
        compiler_params=_cparams(1),
        name="in_proj",
    )(x2d, gain, w_na, w_dil, w_gate, nna, ndil, rot_a, rot_b, bd)


def _na_row_start(r, rows):
    return jnp.clip(r - NA_WIN_ROWS // 2, 0, rows - NA_WIN_ROWS)


def _na_kernel(q_ref, k_ref, v_ref, b_ref, o_ref, *, rows):
    r = pl.program_id(1)
    off = pl.multiple_of(_na_row_start(r, rows) * GRID_W, GRID_W)
    nk = NA_WIN_ROWS * GRID_W
    kw = k_ref[pl.ds(off, nk), :]
    vw = v_ref[pl.ds(off, nk), :]
    q = q_ref[...]
    outs = []
    for h in range(NA_HEADS):
        sl = slice(h * HEAD_DIM, (h + 1) * HEAD_DIM)
        s = _dot_nt(q[:, sl], kw[:, sl]) + b_ref[0, h]
        m = jnp.max(s, axis=-1, keepdims=True)
        p = jnp.exp(s - m)
        l = jnp.sum(p, axis=-1, keepdims=True)
        outs.append(_dot(p.astype(BF16), vw[:, sl]) / l)
    o_ref[...] = jnp.concatenate(outs, axis=1).astype(BF16)


def _na_bias_table(rpb):
    qc = np.arange(GRID_W)[:, None]
    kc = np.arange(GRID_W)[None, :]
    cs = np.clip(qc - NA_WIN_COLS // 2, 0, GRID_W - NA_WIN_COLS)
    mask = (kc >= cs) & (kc < cs + NA_WIN_COLS)
    dc = np.clip(kc - qc + NA_WIN_COLS - 1, 0, 2 * NA_WIN_COLS - 2)
    colb = jnp.where(mask[None, None], rpb.astype(F32)[:, :, dc], NEG_BIG)
    H = rpb.shape[0]
    tabs = [colb[:, s:s + NA_WIN_ROWS].transpose(0, 2, 1, 3).reshape(H, GRID_W, NA_WIN_ROWS * GRID_W)
            for s in range(NA_WIN_ROWS)]
    return jnp.stack(tabs)


def _na_attn(qa, ka, va, bias_tab, batch, seq):
    T, W = qa.shape
    rows = seq // GRID_W

    def bias_idx(b, r):
        return (_na_row_start(r, rows) - r + NA_WIN_ROWS - 1, 0, 0, 0)

    kv = pl.BlockSpec((seq, W), lambda b, r: (b, 0))
    return pl.pallas_call(
        functools.partial(_na_kernel, rows=rows),
        grid=(batch, rows),
        in_specs=[pl.BlockSpec((GRID_W, W), lambda b, r: (b * rows + r, 0)), kv, kv,
                  pl.BlockSpec((1,) + bias_tab.shape[1:], bias_idx)],
        out_specs=pl.BlockSpec((GRID_W, W), lambda b, r: (b * rows + r, 0)),
        out_shape=jax.ShapeDtypeStruct((T, W), BF16),
        compiler_params=_cparams(2),
        name="na_attn",
    )(qa, ka, va, bias_tab)


def _dil_kernel(q_ref, k_ref, v_ref, o_ref, lse_ref, *, length, side, qb):
    kb = qb + 2 * side
    nblk = length // qb
    qi = lax.broadcasted_iota(jnp.int32, (qb, kb), 0)
    kj = lax.broadcasted_iota(jnp.int32, (qb, kb), 1)
    rel = kj - qi

    def block(i, _):
        qs = pl.multiple_of(i * qb, qb)
        ws = pl.multiple_of(jnp.clip(qs - side, 0, length - kb), side)
        delta = rel + (ws - qs)
        band = (delta >= -side) & (delta <= side)
        q = q_ref[0, pl.ds(qs, qb), :]
        k = k_ref[0, pl.ds(ws, kb), :]
        v = v_ref[0, pl.ds(ws, kb), :]
        outs, lses = [], []
        for h in range(LANES // HEAD_DIM):
            sl = slice(h * HEAD_DIM, (h + 1) * HEAD_DIM)
            s = jnp.where(band, _dot_nt(q[:, sl], k[:, sl]), NEG_BIG)
            m = jnp.max(s, axis=-1, keepdims=True)
            p = jnp.exp(s - m)
            l = jnp.sum(p, axis=-1, keepdims=True)
            outs.append(_dot(p.astype(BF16), v[:, sl]) / l)
            lses.append(jnp.broadcast_to(m + jnp.log(l), (qb, HEAD_DIM)))
        o_ref[0, pl.ds(qs, qb), :] = jnp.concatenate(outs, axis=1)
        lse_ref[0, pl.ds(qs, qb), :] = jnp.concatenate(lses, axis=1)
        return 0

    lax.fori_loop(0, nblk, block, 0)


def _dil_attn(q, k, v, batch, seq, window, dilation):
    length = seq // dilation
    side = window // (2 * dilation)
    qb = min(256, length - 2 * side)
    width = dilation * DIL_GROUP_WIDTH
    shp = (batch, length, width)
    spec = pl.BlockSpec((1, length, LANES), lambda b, j: (b, 0, j))
    o, lse = pl.pallas_call(
        functools.partial(_dil_kernel, length=length, side=side, qb=qb),
        grid=(batch, width // LANES),
        in_specs=[spec, spec, spec],
        out_specs=[spec, spec],
        out_shape=[jax.ShapeDtypeStruct(shp, F32)] * 2,
        compiler_params=_cparams(2),
        name=f"dil_attn_d{dilation}",
    )(q.reshape(shp), k.reshape(shp), v.reshape(shp))
    return o.reshape(q.shape), lse.reshape(q.shape)


def _merge_kernel(x_ref, ona_ref, o1_ref, o2_ref, o3_ref, l1_ref, l2_ref, l3_ref, sgn_ref, sgd_ref,
                  wna_ref, wdil_ref, wout_ref, g_ref, x1_ref, h2_ref):
    l1, l2, l3 = l1_ref[...], l2_ref[...], l3_ref[...]
    m = jnp.maximum(jnp.maximum(l1, l2), l3)
    w1, w2, w3 = jnp.exp(l1 - m), jnp.exp(l2 - m), jnp.exp(l3 - m)
    od = (w1 * o1_ref[...] + w2 * o2_ref[...] + w3 * o3_ref[...]) / (w1 + w2 + w3)
    merged = (sgn_ref[...].astype(F32) * _dot(ona_ref[...], wna_ref[...])
              + sgd_ref[...].astype(F32) * _dot(od.astype(BF16), wdil_ref[...]))
    x1 = x_ref[...] + _dot(merged.astype(BF16), wout_ref[...])
    x1_ref[...] = x1
    h2_ref[...] = _rms(x1, g_ref[...])


def _merge(x2d, ona, os_, ls_, sgn, sgd, wna, wdil, wout, gain, tm=256):
    T, D = x2d.shape
    row = lambda w: pl.BlockSpec((tm, w), lambda i: (i, 0))
    return pl.pallas_call(
        _merge_kernel,
        grid=(T // tm,),
        in_specs=[row(D), row(ona.shape[1])] + [row(DIL_GROUP_WIDTH)] * 6 + [row(D), row(D),
                  _const_spec(wna.shape), _const_spec(wdil.shape), _const_spec(wout.shape), _const_spec((1, D))],
        out_specs=[row(D), row(D)],
        out_shape=[jax.ShapeDtypeStruct((T, D), F32)] * 2,
        compiler_params=_cparams(1),
        name="merge",
    )(x2d, ona, *os_, *ls_, sgn, sgd, wna, wdil, wout, gain)


def _topk_rows(s, k, payload=None):
    n = s.shape[0]
    row = lax.broadcasted_iota(jnp.int32, s.shape, 0)
    vals, idxs = [], []
    for _ in range(k):
        m = jnp.max(s, axis=0, keepdims=True)
        idx = jnp.min(jnp.where(s == m, row, n), axis=0, keepdims=True)
        sel = row == idx
        vals.append(m)
        idxs.append(idx if payload is None else jnp.max(jnp.where(sel, payload, -1), axis=0, keepdims=True))
        s = jnp.where(sel, -jnp.inf, s)
    return jnp.concatenate(vals, axis=0), jnp.concatenate(idxs, axis=0)


def _peer_topk_kernel(h_ref, wq_ref, k1_ref, k2_ref, e_ref, g_ref):
    q = _dot(h_ref[...].astype(BF16), wq_ref[...])
    half = PEER_QDIM // 2
    s1 = _dot_nt(k1_ref[...], q[:, :half].astype(BF16))
    s2 = _dot_nt(k2_ref[...], q[:, half:].astype(BF16))
    v1, i1 = _topk_rows(s1, PEER_TOPK)
    v2, i2 = _topk_rows(s2, PEER_TOPK)
    cand = jnp.concatenate([v1[i:i + 1] + v2 for i in range(PEER_TOPK)], axis=0)
    cidx = jnp.concatenate([i1[i:i + 1] * PEER_NKEYS + i2 for i in range(PEER_TOPK)], axis=0)
    sc, e = _topk_rows(cand, PEER_TOPK, payload=cidx)
    p = jnp.exp(sc - sc[0:1])
    e_ref[...] = e
    g_ref[...] = p / jnp.sum(p, axis=0, keepdims=True)


def _peer_topk(h2, wq, k1, k2, tm=512):
    T, D = h2.shape
    out = pl.BlockSpec((PEER_TOPK, tm), lambda i, h: (h, i))
    return pl.pallas_call(
        _peer_topk_kernel,
        grid=(T // tm, PEER_HEADS),
        in_specs=[pl.BlockSpec((tm, D), lambda i, h: (i, 0)), pl.BlockSpec((D, PEER_QDIM), lambda i, h: (0, h)),
                  _const_spec(k1.shape), _const_spec(k2.shape)],
        out_specs=[out, out],
        out_shape=[jax.ShapeDtypeStruct((PEER_SLOTS, T), jnp.int32), jax.ShapeDtypeStruct((PEER_SLOTS, T), F32)],
        compiler_params=_cparams(2),
        name="peer_topk",
    )(h2, wq, k1, k2)


_BITREV3 = (0, 4, 2, 6, 1, 5, 3, 7)


def _sublane_sums(ps):
    sub = lax.broadcasted_iota(jnp.int32, (SUBLANES, LANES), 0)
    ps = [ps[j] for j in _BITREV3]
    m4 = sub < 4
    r = [jnp.where(m4, a, b) + pltpu.roll(jnp.where(m4, b, a), 4, 0) for a, b in zip(ps[0::2], ps[1::2])]
    m2 = (sub % 4) < 2
    s = [jnp.where(m2, a, b) + jnp.where(m2, pltpu.roll(a, 6, 0), pltpu.roll(b, 2, 0)) for a, b in zip(r[0::2], r[1::2])]
    m1 = (sub % 2) < 1
    return jnp.where(m1, s[0], s[1]) + jnp.where(m1, pltpu.roll(s[0], 7, 0), pltpu.roll(s[1], 1, 0))


def _peer_u_kernel(idx_ref, h_ref, tab_ref, g_ref, act_ref, *, tb):
    lane = lax.broadcasted_iota(jnp.int32, (PEER_SLOTS, tb), 1)

    def token(t, a_t):
        hrow = h_ref[t]
        qs = []
        for j0 in range(0, PEER_SLOTS, SUBLANES):
            prods = [tab_ref[idx_ref[t, j0 + j]].astype(F32) * hrow for j in range(SUBLANES)]
            qs.append(_sublane_sums(prods))
        col = jnp.sum(jnp.concatenate(qs, axis=0), axis=1, keepdims=True)
        return jnp.where(lane == t, col, a_t)

    a_t = lax.fori_loop(0, tb, token, jnp.zeros((PEER_SLOTS, tb), F32))
    gelu = 0.5 * a_t * (1.0 + lax.erf(a_t * np.float32(np.sqrt(0.5))))
    act_ref[...] = (gelu * g_ref[...]).T


def _peer_u(idx, h3, tab, gates, tb=128):
    T = idx.shape[0]
    return pl.pallas_call(
        functools.partial(_peer_u_kernel, tb=tb),
        grid=(T // tb,),
        in_specs=[pl.BlockSpec((tb, PEER_SLOTS), lambda i: (i, 0), memory_space=pltpu.SMEM),
                  pl.BlockSpec((tb, SUBLANES, LANES), lambda i: (i, 0, 0)), _const_spec(tab.shape),
                  pl.BlockSpec((PEER_SLOTS, tb), lambda i: (0, i))],
        out_specs=pl.BlockSpec((tb, PEER_SLOTS), lambda i: (i, 0)),
        out_shape=jax.ShapeDtypeStruct((T, PEER_SLOTS), F32),
        compiler_params=_cparams(1),
        name="peer_u",
    )(idx, h3, tab, gates)


def _peer_v_kernel(idx_ref, act_ref, x_ref, tab_ref, o_ref, *, tb):
    nacc = 4

    def token(t, _):
        accs = [jnp.zeros((SUBLANES, LANES), F32)] * nacc
        for j in range(PEER_SLOTS):
            accs[j % nacc] = accs[j % nacc] + act_ref[t, j] * tab_ref[idx_ref[t, j]].astype(F32)
        o_ref[t] = x_ref[t] + ((accs[0] + accs[1]) + (accs[2] + accs[3]))
        return 0

    lax.fori_loop(0, tb, token, 0)


def _peer_v(idx, act, x3, tab, tb=128):
    T = idx.shape[0]
    smem = pl.BlockSpec((tb, PEER_SLOTS), lambda i: (i, 0), memory_space=pltpu.SMEM)
    tok = pl.BlockSpec((tb, SUBLANES, LANES), lambda i: (i, 0, 0))
    return pl.pallas_call(
        functools.partial(_peer_v_kernel, tb=tb),
        grid=(T // tb,),
        in_specs=[smem, smem, tok, _const_spec(tab.shape)],
        out_specs=tok,
        out_shape=jax.ShapeDtypeStruct(x3.shape, F32),
        compiler_params=_cparams(1),
        name="peer_v",
    )(idx, act, x3, tab)


def _ple_kernel(x_ref, p_ref, g_ref, wg_ref, wp_ref, o_ref):
    x = x_ref[...]
    h = _rms(x, g_ref[...]).astype(BF16)
    o_ref[...] = x + jax.nn.sigmoid(_dot(h, wg_ref[...])) * _dot(p_ref[...].astype(BF16), wp_ref[...])


def _ple(x2d, p2d, gain, wg, wp, tm=256):
    T, D = x2d.shape
    row = lambda w: pl.BlockSpec((tm, w), lambda i: (i, 0))
    return pl.pallas_call(
        _ple_kernel,
        grid=(T // tm,),
        in_specs=[row(D), row(p2d.shape[1]), _const_spec((1, D)), _const_spec(wg.shape), _const_spec(wp.shape)],
        out_specs=row(D),
        out_shape=jax.ShapeDtypeStruct((T, D), F32),
        compiler_params=_cparams(1),
        name="ple",
    )(x2d, p2d, gain, wg, wp)


def _rotary_tables(seq):
    half = ROT_DIM // 2
    inv_freq = jnp.power(jnp.float32(ROPE_THETA), -jnp.arange(half, dtype=F32) * 2.0 / ROT_DIM)
    ang = jnp.arange(seq).astype(F32)[:, None] * inv_freq[None, :]
    cos, sin = jnp.cos(ang), jnp.sin(ang)
    pad = HEAD_DIM - ROT_DIM
    ra = jnp.concatenate([cos, cos, jnp.ones((seq, pad), F32)], axis=1)
    rb = jnp.concatenate([-sin, sin, jnp.zeros((seq, pad), F32)], axis=1)
    return jnp.tile(ra, (1, DIL_HEADS_PER_GROUP)), jnp.tile(rb, (1, DIL_HEADS_PER_GROUP))


def kernel(x, p, norm_mix, w_in, qk_norm_na, na_rel_bias, qk_norm_dil, w_branch_na, w_branch_dil, w_out, norm_ffn,
           peer_w_query, peer_sub_keys, peer_expert_u, peer_expert_v, norm_ple, w_ple_gate, w_ple):
    B, S, D = x.shape
    T = B * S
    depth = w_in.shape[0]
    wna = NA_HEADS * HEAD_DIM
    wdil = len(DIL_CONFIGS) * DIL_GROUP_WIDTH
    rot_a, rot_b = _rotary_tables(S)
    x2d = x.reshape(T, D)
    for i in range(depth):
        wi = w_in[i].astype(BF16)
        nna = jnp.tile(qk_norm_na[i], (1, NA_HEADS))
        ndil = jnp.tile(qk_norm_dil[i], (1, DIL_HEADS_PER_GROUP))
        (qa, ka, va, q1, k1, v1, q2, k2, v2, q3, k3, v3, sgn, sgd) = _in_proj(
            x2d, norm_mix[i][None], wi[:, :3 * wna], wi[:, 3 * wna:3 * wna + 3 * wdil], wi[:, 3 * wna + 3 * wdil:],
            nna, ndil, rot_a, rot_b, S)

        ona = _na_attn(qa, ka, va, _na_bias_table(na_rel_bias[i]), B, S)
        dil = [_dil_attn(q, k, v, B, S, window, dilation)
               for (q, k, v), (window, dilation) in zip(((q1, k1, v1), (q2, k2, v2), (q3, k3, v3)), DIL_CONFIGS)]

        x1, h2 = _merge(x2d, ona, [o for o, _ in dil], [l for _, l in dil], sgn, sgd,
                        w_branch_na[i].astype(BF16), w_branch_dil[i].astype(BF16), w_out[i].astype(BF16),
                        norm_ffn[i][None])

        e_t, g_t = _peer_topk(h2, peer_w_query[i].astype(BF16), peer_sub_keys[i, 0].astype(BF16),
                              peer_sub_keys[i, 1].astype(BF16))
        idx = e_t.T
        nexp = peer_expert_u.shape[1]
        tab_u = peer_expert_u[i].astype(BF16).reshape(nexp, SUBLANES, LANES)
        tab_v = peer_expert_v[i].astype(BF16).reshape(nexp, SUBLANES, LANES)
        act = _peer_u(idx, h2.reshape(T, SUBLANES, LANES), tab_u, g_t)
        x2 = _peer_v(idx, act, x1.reshape(T, SUBLANES, LANES), tab_v).reshape(T, D)

        x2d = _ple(x2, p[i].reshape(T, -1), norm_ple[i][None], w_ple_gate[i].astype(BF16), w_ple[i].astype(BF16))
    return x2d.reshape(B, S, D)
```

```python
import functools

import numpy as np
import jax
import jax.numpy as jnp
from jax import lax
from jax.experimental import pallas as pl
from jax.experimental.pallas import tpu as pltpu

F32 = jnp.float32
BF16 = jnp.bfloat16

HEAD_DIM = 64
GRID_W = 64
NA_HEADS = 8
NA_WIN_ROWS = 8
NA_WIN_COLS = 16
DIL_CONFIGS = ((128, 1), (512, 4), (2048, 16))
DIL_HEADS_PER_GROUP = 4
DIL_GROUP_WIDTH = DIL_HEADS_PER_GROUP * HEAD_DIM
ROT_DIM = HEAD_DIM // 4
ROPE_THETA = 500000.0
PEER_HEADS = 8
PEER_NKEYS = 128
PEER_QDIM = 256
PEER_TOPK = 16
PEER_SLOTS = PEER_HEADS * PEER_TOPK
RMS_EPS = 1e-6
NEG_BIG = -1e30

LANES = 128
SUBLANES = 8
VMEM_LIMIT_BYTES = 48 * 1024 * 1024


def _cparams(n_axes):
    return pltpu.CompilerParams(dimension_semantics=("arbitrary",) * n_axes, vmem_limit_bytes=VMEM_LIMIT_BYTES)


def _const_spec(shape):
    nd = len(shape)
    return pl.BlockSpec(shape, lambda *_: (0,) * nd, pipeline_mode=pl.Buffered(1))


def _rms(x, gain):
    return x * lax.rsqrt(jnp.mean(x * x, axis=-1, keepdims=True) + RMS_EPS) * gain


def _dot(a, b):
    return jnp.dot(a, b, preferred_element_type=F32)


def _dot_nt(a, b):
    return lax.dot_general(a, b, (((1,), (1,)), ((), ())), preferred_element_type=F32)


def _head_rms(q, bd_ref, gain):
    outs = []
    for c in range(q.shape[1] // 256):
        qc = q[:, c * 256:(c + 1) * 256]
        sq = qc * qc
        hi = sq.astype(BF16)
        lo = (sq - hi.astype(F32)).astype(BF16)
        ssq = _dot(hi, bd_ref[...]) + _dot(lo, bd_ref[...])
        outs.append(qc * lax.rsqrt(ssq * (1.0 / HEAD_DIM) + RMS_EPS))
    return jnp.concatenate(outs, axis=1) * gain


def _rotary(q, ra, rb):
    lane = lax.broadcasted_iota(jnp.int32, q.shape, 1) % HEAD_DIM
    partner = jnp.where(lane < ROT_DIM // 2, pltpu.roll(q, 256 - ROT_DIM // 2, 1), pltpu.roll(q, ROT_DIM // 2, 1))
    return q * ra + partner * rb


def _in_proj_kernel(x_ref, g_ref, wna_ref, wdil_ref, wgate_ref, nna_ref, ndil_ref, ra_ref, rb_ref, bd_ref,
                    qa_ref, ka_ref, va_ref,
                    q1_ref, k1_ref, v1_ref, q2_ref, k2_ref, v2_ref, q3_ref, k3_ref, v3_ref,
                    sgn_ref, sgd_ref):
    h = _rms(x_ref[...], g_ref[...]).astype(BF16)
    scale = HEAD_DIM ** -0.5

    na = _dot(h, wna_ref[...])
    wna = NA_HEADS * HEAD_DIM
    qa_ref[...] = (_head_rms(na[:, :wna], bd_ref, nna_ref[0:1, :]) * scale).astype(BF16)
    ka_ref[...] = _head_rms(na[:, wna:2 * wna], bd_ref, nna_ref[1:2, :]).astype(BF16)
    va_ref[...] = na[:, 2 * wna:].astype(BF16)

    dil = _dot(h, wdil_ref[...])
    wd = len(DIL_CONFIGS) * DIL_GROUP_WIDTH
    ra = ra_ref[...]
    rb = rb_ref[...]
    q_refs = (q1_ref, q2_ref, q3_ref)
    k_refs = (k1_ref, k2_ref, k3_ref)
    v_refs = (v1_ref, v2_ref, v3_ref)
    for g in range(len(DIL_CONFIGS)):
        lo = g * DIL_GROUP_WIDTH
        hi = lo + DIL_GROUP_WIDTH
        q = _head_rms(dil[:, lo:hi], bd_ref, ndil_ref[0:1, :])
        k = _head_rms(dil[:, wd + lo:wd + hi], bd_ref, ndil_ref[1:2, :])
        q_refs[g][...] = (_rotary(q, ra, rb) * scale).astype(BF16)
        k_refs[g][...] = _rotary(k, ra, rb).astype(BF16)
        v_refs[g][...] = dil[:, 2 * wd + lo:2 * wd + hi].astype(BF16)

    gate = _dot(h, wgate_ref[...])
    d = sgn_ref.shape[1]
    sgn_ref[...] = jax.nn.sigmoid(gate[:, :d]).astype(BF16)
    sgd_ref[...] = jax.nn.sigmoid(gate[:, d:]).astype(BF16)


def _in_proj(x2d, gain, w_na, w_dil, w_gate, nna, ndil, rot_a, rot_b, seq, tm=256):
    T, D = x2d.shape
    bd = jnp.asarray(np.kron(np.eye(256 // HEAD_DIM), np.ones((HEAD_DIM, HEAD_DIM))), BF16)
    wna = NA_HEADS * HEAD_DIM
    nseq = seq // tm
    row = lambda w: pl.BlockSpec((tm, w), lambda i: (i, 0))
    rot = pl.BlockSpec((tm, DIL_GROUP_WIDTH), lambda i: (i % nseq, 0))
    outs = [jax.ShapeDtypeStruct((T, wna), BF16)] * 3 + [jax.ShapeDtypeStruct((T, DIL_GROUP_WIDTH), BF16)] * 9 \
        + [jax.ShapeDtypeStruct((T, D), BF16)] * 2
    return pl.pallas_call(
        _in_proj_kernel,
        grid=(T // tm,),
        in_specs=[row(D), _const_spec((1, D)), _const_spec(w_na.shape), _const_spec(w_dil.shape),
                  _const_spec(w_gate.shape), _const_spec(nna.shape), _const_spec(ndil.shape), rot, rot,
                  _const_spec(bd.shape)],
        out_specs=[row(wna)] * 3 + [row(DIL_GROUP_WIDTH)] * 9 + [row(D)] * 2,
        out_shape=outs,
        compiler_params=_cparams(1),
        name="in_proj",
    )(x2d, gain, w_na, w_dil, w_gate, nna, ndil, rot_a, rot_b, bd)


def _na_row_start(r, rows):
    return jnp.clip(r - NA_WIN_ROWS // 2, 0, rows - NA_WIN_ROWS)


def _na_kernel(q_ref, k_ref, v_ref, b_ref, o_ref, *, rows):
    r = pl.program_id(1)
    off = pl.multiple_of(_na_row_start(r, rows) * GRID_W, GRID_W)
    nk = NA_WIN_ROWS * GRID_W
    kw = k_ref[pl.ds(off, nk), :]
    vw = v_ref[pl.ds(off, nk), :]
    q = q_ref[...]
    outs = []
    for h in range(NA_HEADS):
        sl = slice(h * HEAD_DIM, (h + 1) * HEAD_DIM)
        s = _dot_nt(q[:, sl], kw[:, sl]) + b_ref[0, h]
        m = jnp.max(s, axis=-1, keepdims=True)
        p = jnp.exp(s - m)
        l = jnp.sum(p, axis=-1, keepdims=True)
        outs.append(_dot(p.astype(BF16), vw[:, sl]) / l)
    o_ref[...] = jnp.concatenate(outs, axis=1).astype(BF16)


def _na_bias_table(rpb):
    qc = np.arange(GRID_W)[:, None]
    kc = np.arange(GRID_W)[None, :]
    cs = np.clip(qc - NA_WIN_COLS // 2, 0, GRID_W - NA_WIN_COLS)
    mask = (kc >= cs) & (kc < cs + NA_WIN_COLS)
    dc = np.clip(kc - qc + NA_WIN_COLS - 1, 0, 2 * NA_WIN_COLS - 2)
    colb = jnp.where(mask[None, None], rpb.astype(F32)[:, :, dc], NEG_BIG)
    H = rpb.shape[0]
    tabs = [colb[:, s:s + NA_WIN_ROWS].transpose(0, 2, 1, 3).reshape(H, GRID_W, NA_WIN_ROWS * GRID_W)
            for s in range(NA_WIN_ROWS)]
    return jnp.stack(tabs)


def _na_attn(qa, ka, va, bias_tab, batch, seq):
    T, W = qa.shape
    rows = seq // GRID_W

    def bias_idx(b, r):
        return (_na_row_start(r, rows) - r + NA_WIN_ROWS - 1, 0, 0, 0)

    kv = pl.BlockSpec((seq, W), lambda b, r: (b, 0))
    return pl.pallas_call(
        functools.partial(_na_kernel, rows=rows),
        grid=(batch, rows),
        in_specs=[pl.BlockSpec((GRID_W, W), lambda b, r: (b * rows + r, 0)), kv, kv,
                  pl.BlockSpec((1,) + bias_tab.shape[1:], bias_idx)],
        out_specs=pl.BlockSpec((GRID_W, W), lambda b, r: (b * rows + r, 0)),
        out_shape=jax.ShapeDtypeStruct((T, W), BF16),
        compiler_params=_cparams(2),
        name="na_attn",
    )(qa, ka, va, bias_tab)


def _dil_kernel(q_ref, k_ref, v_ref, o_ref, lse_ref, *, length, side, qb):
    kb = qb + 2 * side
    nblk = length // qb
    qi = lax.broadcasted_iota(jnp.int32, (qb, kb), 0)
    kj = lax.broadcasted_iota(jnp.int32, (qb, kb), 1)
    rel = kj - qi

    def block(i, _):
        qs = pl.multiple_of(i * qb, qb)
        ws = pl.multiple_of(jnp.clip(qs - side, 0, length - kb), side)
        delta = rel + (ws - qs)
        band = (delta >= -side) & (delta <= side)
        q = q_ref[0, pl.ds(qs, qb), :]
        k = k_ref[0, pl.ds(ws, kb), :]
        v = v_ref[0, pl.ds(ws, kb), :]
        outs, lses = [], []
        for h in range(LANES // HEAD_DIM):
            sl = slice(h * HEAD_DIM, (h + 1) * HEAD_DIM)
            s = jnp.where(band, _dot_nt(q[:, sl], k[:, sl]), NEG_BIG)
            m = jnp.max(s, axis=-1, keepdims=True)
            p = jnp.exp(s - m)
            l = jnp.sum(p, axis=-1, keepdims=True)
            outs.append(_dot(p.astype(BF16), v[:, sl]) / l)
            lses.append(jnp.broadcast_to(m + jnp.log(l), (qb, HEAD_DIM)))
        o_ref[0, pl.ds(qs, qb), :] = jnp.concatenate(outs, axis=1)
        lse_ref[0, pl.ds(qs, qb), :] = jnp.concatenate(lses, axis=1)
        return 0

    lax.fori_loop(0, nblk, block, 0)


def _dil_attn(q, k, v, batch, seq, window, dilation):
    length = seq // dilation
    side = window // (2 * dilation)
    qb = min(256, length - 2 * side)
    width = dilation * DIL_GROUP_WIDTH
    shp = (batch, length, width)
    spec = pl.BlockSpec((1, length, LANES), lambda b, j: (b, 0, j))
    o, lse = pl.pallas_call(
        functools.partial(_dil_kernel, length=length, side=side, qb=qb),
        grid=(batch, width // LANES),
        in_specs=[spec, spec, spec],
        out_specs=[spec, spec],
        out_shape=[jax.ShapeDtypeStruct(shp, F32)] * 2,
        compiler_params=_cparams(2),
        name=f"dil_attn_d{dilation}",
    )(q.reshape(shp), k.reshape(shp), v.reshape(shp))
    return o.reshape(q.shape), lse.reshape(q.shape)


def _merge_kernel(x_ref, ona_ref, o1_ref, o2_ref, o3_ref, l1_ref, l2_ref, l3_ref, sgn_ref, sgd_ref,
                  wna_ref, wdil_ref, wout_ref, g_ref, x1_ref, h2_ref):
    l1, l2, l3 = l1_ref[...], l2_ref[...], l3_ref[...]
    m = jnp.maximum(jnp.maximum(l1, l2), l3)
    w1, w2, w3 = jnp.exp(l1 - m), jnp.exp(l2 - m), jnp.exp(l3 - m)
    od = (w1 * o1_ref[...] + w2 * o2_ref[...] + w3 * o3_ref[...]) / (w1 + w2 + w3)
    merged = (sgn_ref[...].astype(F32) * _dot(ona_ref[...], wna_ref[...])
              + sgd_ref[...].astype(F32) * _dot(od.astype(BF16), wdil_ref[...]))
    x1 = x_ref[...] + _dot(merged.astype(BF16), wout_ref[...])
    x1_ref[...] = x1
    h2_ref[...] = _rms(x1, g_ref[...])


def _merge(x2d, ona, os_, ls_, sgn, sgd, wna, wdil, wout, gain, tm=256):
    T, D = x2d.shape
    row = lambda w: pl.BlockSpec((tm, w), lambda i: (i, 0))
    return pl.pallas_call(
        _merge_kernel,
        grid=(T // tm,),
        in_specs=[row(D), row(ona.shape[1])] + [row(DIL_GROUP_WIDTH)] * 6 + [row(D), row(D),
                  _const_spec(wna.shape), _const_spec(wdil.shape), _const_spec(wout.shape), _const_spec((1, D))],
        out_specs=[row(D), row(D)],
        out_shape=[jax.ShapeDtypeStruct((T, D), F32)] * 2,
        compiler_params=_cparams(1),
        name="merge",
    )(x2d, ona, *os_, *ls_, sgn, sgd, wna, wdil, wout, gain)


def _topk_rows(s, k, payload=None):
    n = s.shape[0]
    row = lax.broadcasted_iota(jnp.int32, s.shape, 0)
    vals, idxs = [], []
    for _ in range(k):
        m = jnp.max(s, axis=0, keepdims=True)
        idx = jnp.min(jnp.where(s == m, row, n), axis=0, keepdims=True)
        sel = row == idx
        vals.append(m)
        idxs.append(idx if payload is None else jnp.max(jnp.where(sel, payload, -1), axis=0, keepdims=True))
        s = jnp.where(sel, -jnp.inf, s)
    return jnp.concatenate(vals, axis=0), jnp.concatenate(idxs, axis=0)


def _peer_topk_kernel(h_ref, wq_ref, k1_ref, k2_ref, e_ref, g_ref):
    q = _dot(h_ref[...].astype(BF16), wq_ref[...])
    half = PEER_QDIM // 2
    s1 = _dot_nt(k1_ref[...], q[:, :half].astype(BF16))
    s2 = _dot_nt(k2_ref[...], q[:, half:].astype(BF16))
    v1, i1 = _topk_rows(s1, PEER_TOPK)
    v2, i2 = _topk_rows(s2, PEER_TOPK)
    cand = jnp.concatenate([v1[i:i + 1] + v2 for i in range(PEER_TOPK)], axis=0)
    cidx = jnp.concatenate([i1[i:i + 1] * PEER_NKEYS + i2 for i in range(PEER_TOPK)], axis=0)
    sc, e = _topk_rows(cand, PEER_TOPK, payload=cidx)
    p = jnp.exp(sc - sc[0:1])
    e_ref[...] = e
    g_ref[...] = p / jnp.sum(p, axis=0, keepdims=True)


def _peer_topk(h2, wq, k1, k2, tm=512):
    T, D = h2.shape
    out = pl.BlockSpec((PEER_TOPK, tm), lambda i, h: (h, i))
    return pl.pallas_call(
        _peer_topk_kernel,
        grid=(T // tm, PEER_HEADS),
        in_specs=[pl.BlockSpec((tm, D), lambda i, h: (i, 0)), pl.BlockSpec((D, PEER_QDIM), lambda i, h: (0, h)),
                  _const_spec(k1.shape), _const_spec(k2.shape)],
        out_specs=[out, out],
        out_shape=[jax.ShapeDtypeStruct((PEER_SLOTS, T), jnp.int32), jax.ShapeDtypeStruct((PEER_SLOTS, T), F32)],
        compiler_params=_cparams(2),
        name="peer_topk",
    )(h2, wq, k1, k2)


WORD_ROWS = 4
GATHER_PITCH = 136
HALF = 512


def _pack_table(w):
    e, d = w.shape
    wb = w.astype(BF16)
    pairs = jnp.stack([wb[:, :HALF], wb[:, HALF:]], axis=-1)
    return lax.bitcast_convert_type(pairs, jnp.uint32).reshape(e * WORD_ROWS, LANES)


def _gather_rows(idx_ref, t, tab_ref, gbuf):
    for j in range(PEER_SLOTS):
        row0 = pl.multiple_of(idx_ref[t, j], WORD_ROWS)
        gbuf[pl.ds(j, WORD_ROWS, stride=GATHER_PITCH), :] = tab_ref[pl.ds(row0, WORD_ROWS), :]


def _gathered_matrix(gbuf):
    planes = [pltpu.bitcast(gbuf[i * GATHER_PITCH:i * GATHER_PITCH + PEER_SLOTS, :], BF16) for i in range(WORD_ROWS)]
    return jnp.concatenate(planes, axis=1)


def _lhs16(rows, width):
    r = lax.broadcasted_iota(jnp.int32, (16, width), 0)
    out = jnp.zeros((16, width), F32)
    for k, v in enumerate(rows):
        out = jnp.where(r == k, v, out)
    return out.astype(BF16)


def _pipelined_tokens(tb, gather, compute, bufs):
    (a0, a1), (b0, b1) = bufs
    gather(0, a0)
    gather(1, a1)

    def step(q, _):
        t0 = 4 * q
        gather(t0 + 2, b0)
        gather(t0 + 3, b1)
        compute(t0, a0)
        compute(t0 + 1, a1)
        gather(jnp.minimum(t0 + 4, tb - 1), a0)
        gather(jnp.minimum(t0 + 5, tb - 1), a1)
        compute(t0 + 2, b0)
        compute(t0 + 3, b1)
        return 0

    lax.fori_loop(0, tb // 4, step, 0)


def _peer_u_kernel(idx_ref, h_ref, tab_ref, g_ref, act_ref, hhi_ref, hlo_ref, c_ref, ga0, ga1, gb0, gb1, *, tb):
    h = h_ref[...]
    hi = h.astype(BF16).astype(F32)
    hhi_ref[...] = hi
    hlo_ref[...] = h - hi
    even = (lax.broadcasted_iota(jnp.int32, (SUBLANES, 2 * PEER_SLOTS), 1) % 2) == 0

    def compute(t, gbuf):
        row = pl.ds(t, 1)
        lhs = _lhs16([hhi_ref[row, :HALF], hhi_ref[row, HALF:], hlo_ref[row, :HALF], hlo_ref[row, HALF:]], HALF)
        res = _dot_nt(lhs, _gathered_matrix(gbuf))[:SUBLANES]
        s = res + pltpu.roll(res, 6, 0)
        c_ref[row, :] = jnp.where(even, s, pltpu.roll(s, 7, 0))[0:1]

    _pipelined_tokens(tb, lambda t, g: _gather_rows(idx_ref, t, tab_ref, g), compute, ((ga0, ga1), (gb0, gb1)))

    c = c_ref[...]
    a = c + pltpu.roll(c, 2 * PEER_SLOTS - 1, 1)
    gelu = 0.5 * a * (1.0 + lax.erf(a * np.float32(np.sqrt(0.5))))
    lane_even = (lax.broadcasted_iota(jnp.int32, a.shape, 1) % 2) == 0
    act_ref[...] = jnp.where(lane_even, gelu * g_ref[...], 0.0)


def _gather_scratch():
    return [pltpu.VMEM((WORD_ROWS * GATHER_PITCH, LANES), jnp.uint32)] * 4


def _peer_u(idx, h2, tab, gates, tb=128):
    T, D = h2.shape
    wide = pl.BlockSpec((tb, 2 * PEER_SLOTS), lambda i: (i, 0))
    return pl.pallas_call(
        functools.partial(_peer_u_kernel, tb=tb),
        grid=(T // tb,),
        in_specs=[pl.BlockSpec((tb, PEER_SLOTS), lambda i: (i, 0), memory_space=pltpu.SMEM),
                  pl.BlockSpec((tb, D), lambda i: (i, 0)), _const_spec(tab.shape), wide],
        out_specs=wide,
        out_shape=jax.ShapeDtypeStruct((T, 2 * PEER_SLOTS), F32),
        scratch_shapes=[pltpu.VMEM((tb, D), F32)] * 2 + [pltpu.VMEM((tb, 2 * PEER_SLOTS), F32)] + _gather_scratch(),
        compiler_params=_cparams(1),
        name="peer_u",
    )(idx, h2, tab, gates)


def _peer_v_kernel(idx_ref, act_ref, x_ref, tab_ref, o_ref, ahi_ref, alo_ref, bhi_ref, blo_ref, acc_ref,
                   ga0, ga1, gb0, gb1, *, tb):
    a = act_ref[...]
    b = pltpu.roll(a, 1, 1)
    for src, hi_ref, lo_ref in ((a, ahi_ref, alo_ref), (b, bhi_ref, blo_ref)):
        hi = src.astype(BF16).astype(F32)
        hi_ref[...] = hi
        lo_ref[...] = src - hi

    def compute(t, gbuf):
        row = pl.ds(t, 1)
        lhs = _lhs16([ahi_ref[row, :], bhi_ref[row, :], alo_ref[row, :], blo_ref[row, :]], 2 * PEER_SLOTS)
        res = _dot(lhs, _gathered_matrix(gbuf))[:SUBLANES]
        s = res + pltpu.roll(res, 6, 0)
        acc_ref[row, :HALF] = s[0:1]
        acc_ref[row, HALF:] = s[1:2]

    _pipelined_tokens(tb, lambda t, g: _gather_rows(idx_ref, t, tab_ref, g), compute, ((ga0, ga1), (gb0, gb1)))
    o_ref[...] = x_ref[...] + acc_ref[...]


def _peer_v(idx, act, x2d, tab, tb=128):
    T, D = x2d.shape
    tok = pl.BlockSpec((tb, D), lambda i: (i, 0))
    wide = pl.BlockSpec((tb, 2 * PEER_SLOTS), lambda i: (i, 0))
    return pl.pallas_call(
        functools.partial(_peer_v_kernel, tb=tb),
        grid=(T // tb,),
        in_specs=[pl.BlockSpec((tb, PEER_SLOTS), lambda i: (i, 0), memory_space=pltpu.SMEM), wide, tok,
                  _const_spec(tab.shape)],
        out_specs=tok,
        out_shape=jax.ShapeDtypeStruct((T, D), F32),
        scratch_shapes=[pltpu.VMEM((tb, 2 * PEER_SLOTS), F32)] * 4 + [pltpu.VMEM((tb, D), F32)] + _gather_scratch(),
        compiler_params=_cparams(1),
        name="peer_v",
    )(idx, act, x2d, tab)


def _ple_kernel(x_ref, p_ref, g_ref, wg_ref, wp_ref, o_ref):
    x = x_ref[...]
    h = _rms(x, g_ref[...]).astype(BF16)
    o_ref[...] = x + jax.nn.sigmoid(_dot(h, wg_ref[...])) * _dot(p_ref[...].astype(BF16), wp_ref[...])


def _ple(x2d, p2d, gain, wg, wp, tm=256):
    T, D = x2d.shape
    row = lambda w: pl.BlockSpec((tm, w), lambda i: (i, 0))
    return pl.pallas_call(
        _ple_kernel,
        grid=(T // tm,),
        in_specs=[row(D), row(p2d.shape[1]), _const_spec((1, D)), _const_spec(wg.shape), _const_spec(wp.shape)],
        out_specs=row(D),
        out_shape=jax.ShapeDtypeStruct((T, D), F32),
        compiler_params=_cparams(1),
        name="ple",
    )(x2d, p2d, gain, wg, wp)


def _rotary_tables(seq):
    half = ROT_DIM // 2
    inv_freq = jnp.power(jnp.float32(ROPE_THETA), -jnp.arange(half, dtype=F32) * 2.0 / ROT_DIM)
    ang = jnp.arange(seq).astype(F32)[:, None] * inv_freq[None, :]
    cos, sin = jnp.cos(ang), jnp.sin(ang)
    pad = HEAD_DIM - ROT_DIM
    ra = jnp.concatenate([cos, cos, jnp.ones((seq, pad), F32)], axis=1)
    rb = jnp.concatenate([-sin, sin, jnp.zeros((seq, pad), F32)], axis=1)
    return jnp.tile(ra, (1, DIL_HEADS_PER_GROUP)), jnp.tile(rb, (1, DIL_HEADS_PER_GROUP))


def kernel(x, p, norm_mix, w_in, qk_norm_na, na_rel_bias, qk_norm_dil, w_branch_na, w_branch_dil, w_out, norm_ffn,
           peer_w_query, peer_sub_keys, peer_expert_u, peer_expert_v, norm_ple, w_ple_gate, w_ple):
    B, S, D = x.shape
    T = B * S
    depth = w_in.shape[0]
    wna = NA_HEADS * HEAD_DIM
    wdil = len(DIL_CONFIGS) * DIL_GROUP_WIDTH
    rot_a, rot_b = _rotary_tables(S)
    x2d = x.reshape(T, D)
    for i in range(depth):
        wi = w_in[i].astype(BF16)
        nna = jnp.tile(qk_norm_na[i], (1, NA_HEADS))
        ndil = jnp.tile(qk_norm_dil[i], (1, DIL_HEADS_PER_GROUP))
        (qa, ka, va, q1, k1, v1, q2, k2, v2, q3, k3, v3, sgn, sgd) = _in_proj(
            x2d, norm_mix[i][None], wi[:, :3 * wna], wi[:, 3 * wna:3 * wna + 3 * wdil], wi[:, 3 * wna + 3 * wdil:],
            nna, ndil, rot_a, rot_b, S)

        ona = _na_attn(qa, ka, va, _na_bias_table(na_rel_bias[i]), B, S)
        dil = [_dil_attn(q, k, v, B, S, window, dilation)
               for (q, k, v), (window, dilation) in zip(((q1, k1, v1), (q2, k2, v2), (q3, k3, v3)), DIL_CONFIGS)]

        x1, h2 = _merge(x2d, ona, [o for o, _ in dil], [l for _, l in dil], sgn, sgd,
                        w_branch_na[i].astype(BF16), w_branch_dil[i].astype(BF16), w_out[i].astype(BF16),
                        norm_ffn[i][None])

        e_t, g_t = _peer_topk(h2, peer_w_query[i].astype(BF16), peer_sub_keys[i, 0].astype(BF16),
                              peer_sub_keys[i, 1].astype(BF16))
        idx = e_t.T * WORD_ROWS
        gates = jnp.stack([g_t.T, jnp.zeros((T, PEER_SLOTS), F32)], axis=-1).reshape(T, 2 * PEER_SLOTS)
        act = _peer_u(idx, h2, _pack_table(peer_expert_u[i]), gates)
        x2 = _peer_v(idx, act, x1, _pack_table(peer_expert_v[i]))

        x2d = _ple(x2, p[i].reshape(T, -1), norm_ple[i][None], w_ple_gate[i].astype(BF16), w_ple[i].astype(BF16))
    return x2d.reshape(B, S, D)
```

```python
import functools

import numpy as np
import jax
import jax.numpy as jnp
from jax import lax
from jax.experimental import pallas as pl
from jax.experimental.pallas import tpu as pltpu

F32 = jnp.float32
BF16 = jnp.bfloat16

HEAD_DIM = 64
GRID_W = 64
NA_HEADS = 8
NA_WIN_ROWS = 8
NA_WIN_COLS = 16
DIL_CONFIGS = ((128, 1), (512, 4), (2048, 16))
DIL_HEADS_PER_GROUP = 4
DIL_GROUP_WIDTH = DIL_HEADS_PER_GROUP * HEAD_DIM
ROT_DIM = HEAD_DIM // 4
ROPE_THETA = 500000.0
PEER_HEADS = 8
PEER_NKEYS = 128
PEER_QDIM = 256
PEER_TOPK = 16
PEER_SLOTS = PEER_HEADS * PEER_TOPK
RMS_EPS = 1e-6
NEG_BIG = -1e30

LANES = 128
SUBLANES = 8
VMEM_LIMIT_BYTES = 48 * 1024 * 1024


def _cparams(n_axes):
    return pltpu.CompilerParams(dimension_semantics=("arbitrary",) * n_axes, vmem_limit_bytes=VMEM_LIMIT_BYTES)


def _const_spec(shape):
    nd = len(shape)
    return pl.BlockSpec(shape, lambda *_: (0,) * nd, pipeline_mode=pl.Buffered(1))


def _rms(x, gain):
    return x * lax.rsqrt(jnp.mean(x * x, axis=-1, keepdims=True) + RMS_EPS) * gain


def _dot(a, b):
    return jnp.dot(a, b, preferred_element_type=F32)


def _dot_nt(a, b):
    return lax.dot_general(a, b, (((1,), (1,)), ((), ())), preferred_element_type=F32)


def _head_rms(q, bd_ref, gain):
    outs = []
    for c in range(q.shape[1] // 256):
        qc = q[:, c * 256:(c + 1) * 256]
        sq = qc * qc
        hi = sq.astype(BF16)
        lo = (sq - hi.astype(F32)).astype(BF16)
        ssq = _dot(hi, bd_ref[...]) + _dot(lo, bd_ref[...])
        outs.append(qc * lax.rsqrt(ssq * (1.0 / HEAD_DIM) + RMS_EPS))
    return jnp.concatenate(outs, axis=1) * gain


def _rotary(q, ra, rb):
    lane = lax.broadcasted_iota(jnp.int32, q.shape, 1) % HEAD_DIM
    partner = jnp.where(lane < ROT_DIM // 2, pltpu.roll(q, 256 - ROT_DIM // 2, 1), pltpu.roll(q, ROT_DIM // 2, 1))
    return q * ra + partner * rb


def _in_proj_kernel(x_ref, g_ref, wna_ref, wdil_ref, wgate_ref, nna_ref, ndil_ref, ra_ref, rb_ref, bd_ref,
                    qa_ref, ka_ref, va_ref,
                    q1_ref, k1_ref, v1_ref, q2_ref, k2_ref, v2_ref, q3_ref, k3_ref, v3_ref,
                    sgn_ref, sgd_ref):
    h = _rms(x_ref[...], g_ref[...]).astype(BF16)
    scale = HEAD_DIM ** -0.5

    na = _dot(h, wna_ref[...])
    wna = NA_HEADS * HEAD_DIM
    qa_ref[...] = (_head_rms(na[:, :wna], bd_ref, nna_ref[0:1, :]) * scale).astype(BF16)
    ka_ref[...] = _head_rms(na[:, wna:2 * wna], bd_ref, nna_ref[1:2, :]).astype(BF16)
    va_ref[...] = na[:, 2 * wna:].astype(BF16)

    dil = _dot(h, wdil_ref[...])
    wd = len(DIL_CONFIGS) * DIL_GROUP_WIDTH
    ra = ra_ref[...]
    rb = rb_ref[...]
    q_refs = (q1_ref, q2_ref, q3_ref)
    k_refs = (k1_ref, k2_ref, k3_ref)
    v_refs = (v1_ref, v2_ref, v3_ref)
    for g in range(len(DIL_CONFIGS)):
        lo = g * DIL_GROUP_WIDTH
        hi = lo + DIL_GROUP_WIDTH
        q = _head_rms(dil[:, lo:hi], bd_ref, ndil_ref[0:1, :])
        k = _head_rms(dil[:, wd + lo:wd + hi], bd_ref, ndil_ref[1:2, :])
        q_refs[g][...] = (_rotary(q, ra, rb) * scale).astype(BF16)
        k_refs[g][...] = _rotary(k, ra, rb).astype(BF16)
        v_refs[g][...] = dil[:, 2 * wd + lo:2 * wd + hi].astype(BF16)

    gate = _dot(h, wgate_ref[...])
    d = sgn_ref.shape[1]
    sgn_ref[...] = jax.nn.sigmoid(gate[:, :d]).astype(BF16)
    sgd_ref[...] = jax.nn.sigmoid(gate[:, d:]).astype(BF16)


def _in_proj(x2d, gain, w_na, w_dil, w_gate, nna, ndil, rot_a, rot_b, seq, tm=256):
    T, D = x2d.shape
    bd = jnp.asarray(np.kron(np.eye(256 // HEAD_DIM), np.ones((HEAD_DIM, HEAD_DIM))), BF16)
    wna = NA_HEADS * HEAD_DIM
    nseq = seq // tm
    row = lambda w: pl.BlockSpec((tm, w), lambda i: (i, 0))
    rot = pl.BlockSpec((tm, DIL_GROUP_WIDTH), lambda i: (i % nseq, 0))
    outs = [jax.ShapeDtypeStruct((T, wna), BF16)] * 3 + [jax.ShapeDtypeStruct((T, DIL_GROUP_WIDTH), BF16)] * 9 \
        + [jax.ShapeDtypeStruct((T, D), BF16)] * 2
    return pl.pallas_call(
        _in_proj_kernel,
        grid=(T // tm,),
        in_specs=[row(D), _const_spec((1, D)), _const_spec(w_na.shape), _const_spec(w_dil.shape),
                  _const_spec(w_gate.shape), _const_spec(nna.shape), _const_spec(ndil.shape), rot, rot,
                  _const_spec(bd.shape)],
        out_specs=[row(wna)] * 3 + [row(DIL_GROUP_WIDTH)] * 9 + [row(D)] * 2,
        out_shape=outs,
        compiler_params=_cparams(1),
        name="in_proj",
    )(x2d, gain, w_na, w_dil, w_gate, nna, ndil, rot_a, rot_b, bd)


def _na_row_start(r, rows):
    return jnp.clip(r - NA_WIN_ROWS // 2, 0, rows - NA_WIN_ROWS)


NA_ROWS_PER_STEP = 2


def _head_pair_rows(x, first):
    zero = jnp.zeros_like(x)
    return jnp.concatenate([jnp.where(first, x, zero), jnp.where(first, zero, x)], axis=0)


def _na_kernel(q_ref, k_ref, v_ref, *rest, rows):
    b_refs, o_ref = rest[:NA_ROWS_PER_STEP], rest[NA_ROWS_PER_STEP]
    nk = NA_WIN_ROWS * GRID_W
    pair = 2 * HEAD_DIM
    first = lax.broadcasted_iota(jnp.int32, (GRID_W, pair), 1) < HEAD_DIM
    for sub, b_ref in enumerate(b_refs):
        r = pl.program_id(1) * NA_ROWS_PER_STEP + sub
        off = pl.multiple_of(_na_row_start(r, rows) * GRID_W, GRID_W)
        kw = k_ref[pl.ds(off, nk), :]
        vw = v_ref[pl.ds(off, nk), :]
        q = q_ref[sub * GRID_W:(sub + 1) * GRID_W, :]
        outs = []
        for hp in range(NA_HEADS // 2):
            sl = slice(hp * pair, (hp + 1) * pair)
            s = _dot_nt(_head_pair_rows(q[:, sl], first), kw[:, sl]) + b_ref[0, hp]
            m = jnp.max(s, axis=-1, keepdims=True)
            p = jnp.exp(s - m)
            l = jnp.sum(p, axis=-1, keepdims=True)
            o = _dot(p.astype(BF16), vw[:, sl]) / l
            outs.append(jnp.where(first, o[:GRID_W], o[GRID_W:]))
        o_ref[sub * GRID_W:(sub + 1) * GRID_W, :] = jnp.concatenate(outs, axis=1).astype(BF16)


def _na_bias_table(rpb):
    qc = np.arange(GRID_W)[:, None]
    kc = np.arange(GRID_W)[None, :]
    cs = np.clip(qc - NA_WIN_COLS // 2, 0, GRID_W - NA_WIN_COLS)
    mask = (kc >= cs) & (kc < cs + NA_WIN_COLS)
    dc = np.clip(kc - qc + NA_WIN_COLS - 1, 0, 2 * NA_WIN_COLS - 2)
    colb = jnp.where(mask[None, None], rpb.astype(F32)[:, :, dc], NEG_BIG)
    H = rpb.shape[0]
    tabs = [colb[:, s:s + NA_WIN_ROWS].transpose(0, 2, 1, 3).reshape(H // 2, 2 * GRID_W, NA_WIN_ROWS * GRID_W)
            for s in range(NA_WIN_ROWS)]
    return jnp.stack(tabs)


def _na_attn(qa, ka, va, bias_tab, batch, seq):
    T, W = qa.shape
    rows = seq // GRID_W
    steps = rows // NA_ROWS_PER_STEP

    def bias_spec(sub):
        def idx(b, s):
            r = s * NA_ROWS_PER_STEP + sub
            return (_na_row_start(r, rows) - r + NA_WIN_ROWS - 1, 0, 0, 0)
        return pl.BlockSpec((1,) + bias_tab.shape[1:], idx)

    kv = pl.BlockSpec((seq, W), lambda b, s: (b, 0))
    qo = pl.BlockSpec((NA_ROWS_PER_STEP * GRID_W, W), lambda b, s: (b * steps + s, 0))
    return pl.pallas_call(
        functools.partial(_na_kernel, rows=rows),
        grid=(batch, steps),
        in_specs=[qo, kv, kv] + [bias_spec(sub) for sub in range(NA_ROWS_PER_STEP)],
        out_specs=qo,
        out_shape=jax.ShapeDtypeStruct((T, W), BF16),
        compiler_params=_cparams(2),
        name="na_attn",
    )(qa, ka, va, *([bias_tab] * NA_ROWS_PER_STEP))


DIL_QUERY_BLOCK = 128


def _dil_kernel(q_ref, k_ref, v_ref, o_ref, lse_ref, *, length, side, qb):
    kb = qb + 2 * side
    nblk = length // qb
    qi = lax.broadcasted_iota(jnp.int32, (2 * qb, kb), 0) % qb
    kj = lax.broadcasted_iota(jnp.int32, (2 * qb, kb), 1)
    rel = kj - qi
    first = lax.broadcasted_iota(jnp.int32, (qb, LANES), 1) < HEAD_DIM

    def block(i, _):
        qs = pl.multiple_of(i * qb, qb)
        ws = pl.multiple_of(jnp.clip(qs - side, 0, length - kb), side)
        delta = rel + (ws - qs)
        band = (delta >= -side) & (delta <= side)
        q = q_ref[0, pl.ds(qs, qb), :]
        k = k_ref[0, pl.ds(ws, kb), :]
        v = v_ref[0, pl.ds(ws, kb), :]
        s = jnp.where(band, _dot_nt(_head_pair_rows(q, first), k), NEG_BIG)
        m = jnp.max(s, axis=-1, keepdims=True)
        p = jnp.exp(s - m)
        l = jnp.sum(p, axis=-1, keepdims=True)
        o = _dot(p.astype(BF16), v) / l
        lse = jnp.broadcast_to(m + jnp.log(l), (2 * qb, LANES))
        o_ref[0, pl.ds(qs, qb), :] = jnp.where(first, o[:qb], o[qb:])
        lse_ref[0, pl.ds(qs, qb), :] = jnp.where(first, lse[:qb], lse[qb:])
        return 0

    lax.fori_loop(0, nblk, block, 0)


def _dil_attn(q, k, v, batch, seq, window, dilation):
    length = seq // dilation
    side = window // (2 * dilation)
    qb = min(DIL_QUERY_BLOCK, length - 2 * side)
    width = dilation * DIL_GROUP_WIDTH
    shp = (batch, length, width)
    spec = pl.BlockSpec((1, length, LANES), lambda b, j: (b, 0, j))
    o, lse = pl.pallas_call(
        functools.partial(_dil_kernel, length=length, side=side, qb=qb),
        grid=(batch, width // LANES),
        in_specs=[spec, spec, spec],
        out_specs=[spec, spec],
        out_shape=[jax.ShapeDtypeStruct(shp, F32)] * 2,
        compiler_params=_cparams(2),
        name=f"dil_attn_d{dilation}",
    )(q.reshape(shp), k.reshape(shp), v.reshape(shp))
    return o.reshape(q.shape), lse.reshape(q.shape)


def _merge_kernel(x_ref, ona_ref, o1_ref, o2_ref, o3_ref, l1_ref, l2_ref, l3_ref, sgn_ref, sgd_ref,
                  wna_ref, wdil_ref, wout_ref, g_ref, x1_ref, h2_ref):
    l1, l2, l3 = l1_ref[...], l2_ref[...], l3_ref[...]
    m = jnp.maximum(jnp.maximum(l1, l2), l3)
    w1, w2, w3 = jnp.exp(l1 - m), jnp.exp(l2 - m), jnp.exp(l3 - m)
    od = (w1 * o1_ref[...] + w2 * o2_ref[...] + w3 * o3_ref[...]) / (w1 + w2 + w3)
    merged = (sgn_ref[...].astype(F32) * _dot(ona_ref[...], wna_ref[...])
              + sgd_ref[...].astype(F32) * _dot(od.astype(BF16), wdil_ref[...]))
    x1 = x_ref[...] + _dot(merged.astype(BF16), wout_ref[...])
    x1_ref[...] = x1
    h2_ref[...] = _rms(x1, g_ref[...])


def _merge(x2d, ona, os_, ls_, sgn, sgd, wna, wdil, wout, gain, tm=256):
    T, D = x2d.shape
    row = lambda w: pl.BlockSpec((tm, w), lambda i: (i, 0))
    return pl.pallas_call(
        _merge_kernel,
        grid=(T // tm,),
        in_specs=[row(D), row(ona.shape[1])] + [row(DIL_GROUP_WIDTH)] * 6 + [row(D), row(D),
                  _const_spec(wna.shape), _const_spec(wdil.shape), _const_spec(wout.shape), _const_spec((1, D))],
        out_specs=[row(D), row(D)],
        out_shape=[jax.ShapeDtypeStruct((T, D), F32)] * 2,
        compiler_params=_cparams(1),
        name="merge",
    )(x2d, ona, *os_, *ls_, sgn, sgd, wna, wdil, wout, gain)


def _topk_rows(s, k, payload=None):
    n = s.shape[0]
    row = lax.broadcasted_iota(jnp.int32, s.shape, 0)
    vals, idxs = [], []
    for _ in range(k):
        m = jnp.max(s, axis=0, keepdims=True)
        idx = jnp.min(jnp.where(s == m, row, n), axis=0, keepdims=True)
        sel = row == idx
        vals.append(m)
        idxs.append(idx if payload is None else jnp.max(jnp.where(sel, payload, -1), axis=0, keepdims=True))
        s = jnp.where(sel, -jnp.inf, s)
    return jnp.concatenate(vals, axis=0), jnp.concatenate(idxs, axis=0)


def _peer_topk_kernel(h_ref, wq_ref, k1_ref, k2_ref, e_ref, g_ref):
    q = _dot(h_ref[...].astype(BF16), wq_ref[...])
    half = PEER_QDIM // 2
    s1 = _dot_nt(k1_ref[...], q[:, :half].astype(BF16))
    s2 = _dot_nt(k2_ref[...], q[:, half:].astype(BF16))
    v1, i1 = _topk_rows(s1, PEER_TOPK)
    v2, i2 = _topk_rows(s2, PEER_TOPK)
    keep = [PEER_TOPK // (i + 1) for i in range(PEER_TOPK)]
    pad = -sum(keep) % SUBLANES
    tm = v1.shape[1]
    cand = jnp.concatenate([v1[i:i + 1] + v2[:n] for i, n in enumerate(keep)]
                           + [jnp.full((pad, tm), -jnp.inf, F32)], axis=0)
    cidx = jnp.concatenate([i1[i:i + 1] * PEER_NKEYS + i2[:n] for i, n in enumerate(keep)]
                           + [jnp.full((pad, tm), -1, jnp.int32)], axis=0)
    sc, e = _topk_rows(cand, PEER_TOPK, payload=cidx)
    p = jnp.exp(sc - sc[0:1])
    e_ref[...] = e
    g_ref[...] = p / jnp.sum(p, axis=0, keepdims=True)


def _peer_topk(h2, wq, k1, k2, tm=512):
    T, D = h2.shape
    out = pl.BlockSpec((PEER_TOPK, tm), lambda i, h: (h, i))
    return pl.pallas_call(
        _peer_topk_kernel,
        grid=(T // tm, PEER_HEADS),
        in_specs=[pl.BlockSpec((tm, D), lambda i, h: (i, 0)), pl.BlockSpec((D, PEER_QDIM), lambda i, h: (0, h)),
                  _const_spec(k1.shape), _const_spec(k2.shape)],
        out_specs=[out, out],
        out_shape=[jax.ShapeDtypeStruct((PEER_SLOTS, T), jnp.int32), jax.ShapeDtypeStruct((PEER_SLOTS, T), F32)],
        compiler_params=_cparams(2),
        name="peer_topk",
    )(h2, wq, k1, k2)


WORD_ROWS = 4
GATHER_PITCH = 136
HALF = 512


def _pack_table(w):
    e, d = w.shape
    wb = w.astype(BF16)
    pairs = jnp.stack([wb[:, :HALF], wb[:, HALF:]], axis=-1)
    return lax.bitcast_convert_type(pairs, jnp.uint32).reshape(e * WORD_ROWS, LANES)


def _gather_rows(idx_ref, t, tab_ref, gbuf):
    for j in range(PEER_SLOTS):
        row0 = pl.multiple_of(idx_ref[t, j], WORD_ROWS)
        gbuf[pl.ds(j, WORD_ROWS, stride=GATHER_PITCH), :] = tab_ref[pl.ds(row0, WORD_ROWS), :]


def _gathered_matrix(gbuf):
    planes = [pltpu.bitcast(gbuf[i * GATHER_PITCH:i * GATHER_PITCH + PEER_SLOTS, :], BF16) for i in range(WORD_ROWS)]
    return jnp.concatenate(planes, axis=1)


def _lhs16(rows, width):
    r = lax.broadcasted_iota(jnp.int32, (16, width), 0)
    out = jnp.zeros((16, width), F32)
    for k, v in enumerate(rows):
        out = jnp.where(r == k, v, out)
    return out.astype(BF16)


PEER_TOKENS_PER_STEP = 64
PIPELINE_LAG = 2


def _pipelined_tokens(tb, gather, compute, bufs):
    n = len(bufs)
    for t in range(tb + PIPELINE_LAG):
        if t < tb:
            gather(t, bufs[t % n])
        if t >= PIPELINE_LAG:
            compute(t - PIPELINE_LAG, bufs[(t - PIPELINE_LAG) % n])


def _peer_u_kernel(idx_ref, h_ref, tab_ref, g_ref, act_ref, hhi_ref, hlo_ref, c_ref, ga0, ga1, gb0, gb1, *, tb):
    h = h_ref[...]
    hi = h.astype(BF16).astype(F32)
    hhi_ref[...] = hi
    hlo_ref[...] = h - hi
    even = (lax.broadcasted_iota(jnp.int32, (SUBLANES, 2 * PEER_SLOTS), 1) % 2) == 0

    def compute(t, gbuf):
        row = pl.ds(t, 1)
        lhs = _lhs16([hhi_ref[row, :HALF], hhi_ref[row, HALF:], hlo_ref[row, :HALF], hlo_ref[row, HALF:]], HALF)
        res = _dot_nt(lhs, _gathered_matrix(gbuf))[:SUBLANES]
        s = res + pltpu.roll(res, 6, 0)
        c_ref[row, :] = jnp.where(even, s, pltpu.roll(s, 7, 0))[0:1]

    _pipelined_tokens(tb, lambda t, g: _gather_rows(idx_ref, t, tab_ref, g), compute, (ga0, ga1, gb0, gb1))

    c = c_ref[...]
    a = c + pltpu.roll(c, 2 * PEER_SLOTS - 1, 1)
    gelu = 0.5 * a * (1.0 + lax.erf(a * np.float32(np.sqrt(0.5))))
    lane_even = (lax.broadcasted_iota(jnp.int32, a.shape, 1) % 2) == 0
    act_ref[...] = jnp.where(lane_even, gelu * g_ref[...], 0.0)


def _gather_scratch():
    return [pltpu.VMEM((WORD_ROWS * GATHER_PITCH, LANES), jnp.uint32)] * 4


def _peer_u(idx, h2, tab, gates, tb=PEER_TOKENS_PER_STEP):
    T, D = h2.shape
    wide = pl.BlockSpec((tb, 2 * PEER_SLOTS), lambda i: (i, 0))
    return pl.pallas_call(
        functools.partial(_peer_u_kernel, tb=tb),
        grid=(T // tb,),
        in_specs=[pl.BlockSpec((tb, PEER_SLOTS), lambda i: (i, 0), memory_space=pltpu.SMEM),
                  pl.BlockSpec((tb, D), lambda i: (i, 0)), _const_spec(tab.shape), wide],
        out_specs=wide,
        out_shape=jax.ShapeDtypeStruct((T, 2 * PEER_SLOTS), F32),
        scratch_shapes=[pltpu.VMEM((tb, D), F32)] * 2 + [pltpu.VMEM((tb, 2 * PEER_SLOTS), F32)] + _gather_scratch(),
        compiler_params=_cparams(1),
        name="peer_u",
    )(idx, h2, tab, gates)


def _peer_v_kernel(idx_ref, act_ref, x_ref, tab_ref, o_ref, ahi_ref, alo_ref, bhi_ref, blo_ref, acc_ref,
                   ga0, ga1, gb0, gb1, *, tb):
    a = act_ref[...]
    b = pltpu.roll(a, 1, 1)
    for src, hi_ref, lo_ref in ((a, ahi_ref, alo_ref), (b, bhi_ref, blo_ref)):
        hi = src.astype(BF16).astype(F32)
        hi_ref[...] = hi
        lo_ref[...] = src - hi

    def compute(t, gbuf):
        row = pl.ds(t, 1)
        lhs = _lhs16([ahi_ref[row, :], bhi_ref[row, :], alo_ref[row, :], blo_ref[row, :]], 2 * PEER_SLOTS)
        res = _dot(lhs, _gathered_matrix(gbuf))[:SUBLANES]
        s = res + pltpu.roll(res, 6, 0)
        acc_ref[row, :HALF] = s[0:1]
        acc_ref[row, HALF:] = s[1:2]

    _pipelined_tokens(tb, lambda t, g: _gather_rows(idx_ref, t, tab_ref, g), compute, (ga0, ga1, gb0, gb1))
    o_ref[...] = x_ref[...] + acc_ref[...]


def _peer_v(idx, act, x2d, tab, tb=PEER_TOKENS_PER_STEP):
    T, D = x2d.shape
    tok = pl.BlockSpec((tb, D), lambda i: (i, 0))
    wide = pl.BlockSpec((tb, 2 * PEER_SLOTS), lambda i: (i, 0))
    return pl.pallas_call(
        functools.partial(_peer_v_kernel, tb=tb),
        grid=(T // tb,),
        in_specs=[pl.BlockSpec((tb, PEER_SLOTS), lambda i: (i, 0), memory_space=pltpu.SMEM), wide, tok,
                  _const_spec(tab.shape)],
        out_specs=tok,
        out_shape=jax.ShapeDtypeStruct((T, D), F32),
        scratch_shapes=[pltpu.VMEM((tb, 2 * PEER_SLOTS), F32)] * 4 + [pltpu.VMEM((tb, D), F32)] + _gather_scratch(),
        compiler_params=_cparams(1),
        name="peer_v",
    )(idx, act, x2d, tab)


def _ple_kernel(x_ref, p_ref, g_ref, wg_ref, wp_ref, o_ref):
    x = x_ref[...]
    h = _rms(x, g_ref[...]).astype(BF16)
    o_ref[...] = x + jax.nn.sigmoid(_dot(h, wg_ref[...])) * _dot(p_ref[...].astype(BF16), wp_ref[...])


def _ple(x2d, p2d, gain, wg, wp, tm=256):
    T, D = x2d.shape
    row = lambda w: pl.BlockSpec((tm, w), lambda i: (i, 0))
    return pl.pallas_call(
        _ple_kernel,
        grid=(T // tm,),
        in_specs=[row(D), row(p2d.shape[1]), _const_spec((1, D)), _const_spec(wg.shape), _const_spec(wp.shape)],
        out_specs=row(D),
        out_shape=jax.ShapeDtypeStruct((T, D), F32),
        compiler_params=_cparams(1),
        name="ple",
    )(x2d, p2d, gain, wg, wp)


def _rotary_tables(seq):
    half = ROT_DIM // 2
    inv_freq = jnp.power(jnp.float32(ROPE_THETA), -jnp.arange(half, dtype=F32) * 2.0 / ROT_DIM)
    ang = jnp.arange(seq).astype(F32)[:, None] * inv_freq[None, :]
    cos, sin = jnp.cos(ang), jnp.sin(ang)
    pad = HEAD_DIM - ROT_DIM
    ra = jnp.concatenate([cos, cos, jnp.ones((seq, pad), F32)], axis=1)
    rb = jnp.concatenate([-sin, sin, jnp.zeros((seq, pad), F32)], axis=1)
    return jnp.tile(ra, (1, DIL_HEADS_PER_GROUP)), jnp.tile(rb, (1, DIL_HEADS_PER_GROUP))


def kernel(x, p, norm_mix, w_in, qk_norm_na, na_rel_bias, qk_norm_dil, w_branch_na, w_branch_dil, w_out, norm_ffn,
           peer_w_query, peer_sub_keys, peer_expert_u, peer_expert_v, norm_ple, w_ple_gate, w_ple):
    B, S, D = x.shape
    T = B * S
    depth = w_in.shape[0]
    wna = NA_HEADS * HEAD_DIM
    wdil = len(DIL_CONFIGS) * DIL_GROUP_WIDTH
    rot_a, rot_b = _rotary_tables(S)
    x2d = x.reshape(T, D)
    for i in range(depth):
        wi = w_in[i].astype(BF16)
        nna = jnp.tile(qk_norm_na[i], (1, NA_HEADS))
        ndil = jnp.tile(qk_norm_dil[i], (1, DIL_HEADS_PER_GROUP))
        (qa, ka, va, q1, k1, v1, q2, k2, v2, q3, k3, v3, sgn, sgd) = _in_proj(
            x2d, norm_mix[i][None], wi[:, :3 * wna], wi[:, 3 * wna:3 * wna + 3 * wdil], wi[:, 3 * wna + 3 * wdil:],
            nna, ndil, rot_a, rot_b, S)

        ona = _na_attn(qa, ka, va, _na_bias_table(na_rel_bias[i]), B, S)
        dil = [_dil_attn(q, k, v, B, S, window, dilation)
               for (q, k, v), (window, dilation) in zip(((q1, k1, v1), (q2, k2, v2), (q3, k3, v3)), DIL_CONFIGS)]

        x1, h2 = _merge(x2d, ona, [o for o, _ in dil], [l for _, l in dil], sgn, sgd,
                        w_branch_na[i].astype(BF16), w_branch_dil[i].astype(BF16), w_out[i].astype(BF16),
                        norm_ffn[i][None])

        e_t, g_t = _peer_topk(h2, peer_w_query[i].astype(BF16), peer_sub_keys[i, 0].astype(BF16),
                              peer_sub_keys[i, 1].astype(BF16))
        idx = e_t.T * WORD_ROWS
        gates = jnp.stack([g_t.T, jnp.zeros((T, PEER_SLOTS), F32)], axis=-1).reshape(T, 2 * PEER_SLOTS)
        act = _peer_u(idx, h2, _pack_table(peer_expert_u[i]), gates)
        x2 = _peer_v(idx, act, x1, _pack_table(peer_expert_v[i]))

        x2d = _ple(x2, p[i].reshape(T, -1), norm_ple[i][None], w_ple_gate[i].astype(BF16), w_ple[i].astype(BF16))
    return x2d.reshape(B, S, D)
```

```python
import functools

import numpy as np
import jax
import jax.numpy as jnp
from jax import lax
from jax.experimental import pallas as pl
from jax.experimental.pallas import tpu as pltpu

F32 = jnp.float32
BF16 = jnp.bfloat16

HEAD_DIM = 64
GRID_W = 64
NA_HEADS = 8
NA_WIN_ROWS = 8
NA_WIN_COLS = 16
DIL_CONFIGS = ((128, 1), (512, 4), (2048, 16))
DIL_HEADS_PER_GROUP = 4
DIL_GROUP_WIDTH = DIL_HEADS_PER_GROUP * HEAD_DIM
ROT_DIM = HEAD_DIM // 4
ROPE_THETA = 500000.0
PEER_HEADS = 8
PEER_NKEYS = 128
PEER_QDIM = 256
PEER_TOPK = 16
PEER_SLOTS = PEER_HEADS * PEER_TOPK
RMS_EPS = 1e-6
NEG_BIG = -1e30

LANES = 128
SUBLANES = 8
VMEM_LIMIT_BYTES = 48 * 1024 * 1024


def _cparams(n_axes):
    return pltpu.CompilerParams(dimension_semantics=("arbitrary",) * n_axes, vmem_limit_bytes=VMEM_LIMIT_BYTES)


def _const_spec(shape):
    nd = len(shape)
    return pl.BlockSpec(shape, lambda *_: (0,) * nd, pipeline_mode=pl.Buffered(1))


def _rms(x, gain):
    return x * lax.rsqrt(jnp.mean(x * x, axis=-1, keepdims=True) + RMS_EPS) * gain


def _dot(a, b):
    return jnp.dot(a, b, preferred_element_type=F32)


def _dot_nt(a, b):
    return lax.dot_general(a, b, (((1,), (1,)), ((), ())), preferred_element_type=F32)


def _relayout_scratch(tm):
    return pltpu.VMEM((DIL_GROUP_WIDTH // LANES, tm, LANES), F32)


def _to_residue_layout(x, scr_ref, d):
    tm = x.shape[0]
    slabs = DIL_GROUP_WIDTH // LANES
    for c in range(slabs):
        scr_ref[c] = x[:, c * LANES:(c + 1) * LANES]
    return jnp.concatenate([scr_ref[c, pl.ds(r, tm // d, stride=d), :] for r in range(d) for c in range(slabs)], axis=1)


def _from_residue_layout(y, scr_ref, d):
    rows = y.shape[0]
    slabs = DIL_GROUP_WIDTH // LANES
    for r in range(d):
        for c in range(slabs):
            lo = r * DIL_GROUP_WIDTH + c * LANES
            scr_ref[c, pl.ds(r, rows, stride=d), :] = y[:, lo:lo + LANES]
    return jnp.concatenate([scr_ref[c] for c in range(slabs)], axis=1)


def _head_rms(q, bd_ref, gain):
    outs = []
    for c in range(q.shape[1] // 256):
        qc = q[:, c * 256:(c + 1) * 256]
        sq = qc * qc
        hi = sq.astype(BF16)
        lo = (sq - hi.astype(F32)).astype(BF16)
        ssq = _dot(hi, bd_ref[...]) + _dot(lo, bd_ref[...])
        outs.append(qc * lax.rsqrt(ssq * (1.0 / HEAD_DIM) + RMS_EPS))
    return jnp.concatenate(outs, axis=1) * gain


def _rotary(q, ra, rb):
    lane = lax.broadcasted_iota(jnp.int32, q.shape, 1) % HEAD_DIM
    partner = jnp.where(lane < ROT_DIM // 2, pltpu.roll(q, 256 - ROT_DIM // 2, 1), pltpu.roll(q, ROT_DIM // 2, 1))
    return q * ra + partner * rb


def _in_proj_kernel(x_ref, g_ref, wna_ref, wdil_ref, wgate_ref, nna_ref, ndil_ref, ra_ref, rb_ref, bd_ref,
                    qa_ref, ka_ref, va_ref,
                    q1_ref, k1_ref, v1_ref, q2_ref, k2_ref, v2_ref, q3_ref, k3_ref, v3_ref,
                    sgn_ref, sgd_ref, *relayout_scratch):
    scratch = iter(relayout_scratch)
    h = _rms(x_ref[...], g_ref[...]).astype(BF16)
    scale = HEAD_DIM ** -0.5

    na = _dot(h, wna_ref[...])
    wna = NA_HEADS * HEAD_DIM
    qa_ref[...] = (_head_rms(na[:, :wna], bd_ref, nna_ref[0:1, :]) * scale).astype(BF16)
    ka_ref[...] = _head_rms(na[:, wna:2 * wna], bd_ref, nna_ref[1:2, :]).astype(BF16)
    va_ref[...] = na[:, 2 * wna:].astype(BF16)

    dil = _dot(h, wdil_ref[...])
    wd = len(DIL_CONFIGS) * DIL_GROUP_WIDTH
    ra = ra_ref[...]
    rb = rb_ref[...]
    q_refs = (q1_ref, q2_ref, q3_ref)
    k_refs = (k1_ref, k2_ref, k3_ref)
    v_refs = (v1_ref, v2_ref, v3_ref)
    for g in range(len(DIL_CONFIGS)):
        lo = g * DIL_GROUP_WIDTH
        hi = lo + DIL_GROUP_WIDTH
        q = _head_rms(dil[:, lo:hi], bd_ref, ndil_ref[0:1, :])
        k = _head_rms(dil[:, wd + lo:wd + hi], bd_ref, ndil_ref[1:2, :])
        dilation = DIL_CONFIGS[g][1]
        for ref, val in ((q_refs[g], _rotary(q, ra, rb) * scale), (k_refs[g], _rotary(k, ra, rb)),
                         (v_refs[g], dil[:, 2 * wd + lo:2 * wd + hi])):
            if dilation > 1:
                val = _to_residue_layout(val, next(scratch), dilation)
            ref[...] = val.astype(BF16)

    gate = _dot(h, wgate_ref[...])
    d = sgn_ref.shape[1]
    sgn_ref[...] = jax.nn.sigmoid(gate[:, :d]).astype(BF16)
    sgd_ref[...] = jax.nn.sigmoid(gate[:, d:]).astype(BF16)


def _in_proj(x2d, gain, w_na, w_dil, w_gate, nna, ndil, rot_a, rot_b, seq, tm=256):
    T, D = x2d.shape
    bd = jnp.asarray(np.kron(np.eye(256 // HEAD_DIM), np.ones((HEAD_DIM, HEAD_DIM))), BF16)
    wna = NA_HEADS * HEAD_DIM
    nseq = seq // tm
    row = lambda w: pl.BlockSpec((tm, w), lambda i: (i, 0))
    rot = pl.BlockSpec((tm, DIL_GROUP_WIDTH), lambda i: (i % nseq, 0))
    dil_shapes = [(T // d, d * DIL_GROUP_WIDTH) for _, d in DIL_CONFIGS for _ in range(3)]
    dil_specs = [pl.BlockSpec((tm // d, d * DIL_GROUP_WIDTH), lambda i: (i, 0)) for _, d in DIL_CONFIGS for _ in range(3)]
    outs = [jax.ShapeDtypeStruct((T, wna), BF16)] * 3 + [jax.ShapeDtypeStruct(s, BF16) for s in dil_shapes] \
        + [jax.ShapeDtypeStruct((T, D), BF16)] * 2
    n_relayout = 3 * sum(d > 1 for _, d in DIL_CONFIGS)
    return pl.pallas_call(
        _in_proj_kernel,
        grid=(T // tm,),
        in_specs=[row(D), _const_spec((1, D)), _const_spec(w_na.shape), _const_spec(w_dil.shape),
                  _const_spec(w_gate.shape), _const_spec(nna.shape), _const_spec(ndil.shape), rot, rot,
                  _const_spec(bd.shape)],
        out_specs=[row(wna)] * 3 + dil_specs + [row(D)] * 2,
        out_shape=outs,
        scratch_shapes=[_relayout_scratch(tm)] * n_relayout,
        compiler_params=_cparams(1),
        name="in_proj",
    )(x2d, gain, w_na, w_dil, w_gate, nna, ndil, rot_a, rot_b, bd)


def _na_row_start(r, rows):
    return jnp.clip(r - NA_WIN_ROWS // 2, 0, rows - NA_WIN_ROWS)


NA_ROWS_PER_STEP = 2


def _head_pair_rows(x, first):
    zero = jnp.zeros_like(x)
    return jnp.concatenate([jnp.where(first, x, zero), jnp.where(first, zero, x)], axis=0)


def _na_kernel(q_ref, k_ref, v_ref, *rest, rows):
    b_refs, o_ref = rest[:NA_ROWS_PER_STEP], rest[NA_ROWS_PER_STEP]
    nk = NA_WIN_ROWS * GRID_W
    pair = 2 * HEAD_DIM
    first = lax.broadcasted_iota(jnp.int32, (GRID_W, pair), 1) < HEAD_DIM
    for sub, b_ref in enumerate(b_refs):
        r = pl.program_id(1) * NA_ROWS_PER_STEP + sub
        off = pl.multiple_of(_na_row_start(r, rows) * GRID_W, GRID_W)
        kw = k_ref[pl.ds(off, nk), :]
        vw = v_ref[pl.ds(off, nk), :]
        q = q_ref[sub * GRID_W:(sub + 1) * GRID_W, :]
        outs = []
        for hp in range(NA_HEADS // 2):
            sl = slice(hp * pair, (hp + 1) * pair)
            s = _dot_nt(_head_pair_rows(q[:, sl], first), kw[:, sl]) + b_ref[0, hp]
            m = jnp.max(s, axis=-1, keepdims=True)
            p = jnp.exp(s - m)
            l = jnp.sum(p, axis=-1, keepdims=True)
            o = _dot(p.astype(BF16), vw[:, sl]) / l
            outs.append(jnp.where(first, o[:GRID_W], o[GRID_W:]))
        o_ref[sub * GRID_W:(sub + 1) * GRID_W, :] = jnp.concatenate(outs, axis=1).astype(BF16)


def _na_bias_table(rpb):
    qc = np.arange(GRID_W)[:, None]
    kc = np.arange(GRID_W)[None, :]
    cs = np.clip(qc - NA_WIN_COLS // 2, 0, GRID_W - NA_WIN_COLS)
    mask = (kc >= cs) & (kc < cs + NA_WIN_COLS)
    dc = np.clip(kc - qc + NA_WIN_COLS - 1, 0, 2 * NA_WIN_COLS - 2)
    colb = jnp.where(mask[None, None], rpb.astype(F32)[:, :, dc], NEG_BIG)
    H = rpb.shape[0]
    tabs = [colb[:, s:s + NA_WIN_ROWS].transpose(0, 2, 1, 3).reshape(H // 2, 2 * GRID_W, NA_WIN_ROWS * GRID_W)
            for s in range(NA_WIN_ROWS)]
    return jnp.stack(tabs)


def _na_attn(qa, ka, va, bias_tab, batch, seq):
    T, W = qa.shape
    rows = seq // GRID_W
    steps = rows // NA_ROWS_PER_STEP

    def bias_spec(sub):
        def idx(b, s):
            r = s * NA_ROWS_PER_STEP + sub
            return (_na_row_start(r, rows) - r + NA_WIN_ROWS - 1, 0, 0, 0)
        return pl.BlockSpec((1,) + bias_tab.shape[1:], idx)

    kv = pl.BlockSpec((seq, W), lambda b, s: (b, 0))
    qo = pl.BlockSpec((NA_ROWS_PER_STEP * GRID_W, W), lambda b, s: (b * steps + s, 0))
    return pl.pallas_call(
        functools.partial(_na_kernel, rows=rows),
        grid=(batch, steps),
        in_specs=[qo, kv, kv] + [bias_spec(sub) for sub in range(NA_ROWS_PER_STEP)],
        out_specs=qo,
        out_shape=jax.ShapeDtypeStruct((T, W), BF16),
        compiler_params=_cparams(2),
        name="na_attn",
    )(qa, ka, va, *([bias_tab] * NA_ROWS_PER_STEP))


DIL_QUERY_BLOCK = 128


def _dil_kernel(q_ref, k_ref, v_ref, o_ref, lse_ref, *, length, side, qb):
    kb = qb + 2 * side
    nblk = length // qb
    qi = lax.broadcasted_iota(jnp.int32, (2 * qb, kb), 0) % qb
    kj = lax.broadcasted_iota(jnp.int32, (2 * qb, kb), 1)
    rel = kj - qi
    first = lax.broadcasted_iota(jnp.int32, (qb, LANES), 1) < HEAD_DIM

    def block(i, _):
        qs = pl.multiple_of(i * qb, qb)
        ws = pl.multiple_of(jnp.clip(qs - side, 0, length - kb), side)
        delta = rel + (ws - qs)
        band = (delta >= -side) & (delta <= side)
        q = q_ref[0, pl.ds(qs, qb), :]
        k = k_ref[0, pl.ds(ws, kb), :]
        v = v_ref[0, pl.ds(ws, kb), :]
        s = jnp.where(band, _dot_nt(_head_pair_rows(q, first), k), NEG_BIG)
        m = jnp.max(s, axis=-1, keepdims=True)
        p = jnp.exp(s - m)
        l = jnp.sum(p, axis=-1, keepdims=True)
        o = _dot(p.astype(BF16), v) / l
        lse = jnp.broadcast_to(m + jnp.log(l), (2 * qb, LANES))
        o_ref[0, pl.ds(qs, qb), :] = jnp.where(first, o[:qb], o[qb:])
        lse_ref[0, pl.ds(qs, qb), :] = jnp.where(first, lse[:qb], lse[qb:])
        return 0

    lax.fori_loop(0, nblk, block, 0)


def _dil_attn(q, k, v, batch, seq, window, dilation):
    length = seq // dilation
    side = window // (2 * dilation)
    qb = min(DIL_QUERY_BLOCK, length - 2 * side)
    width = dilation * DIL_GROUP_WIDTH
    shp = (batch, length, width)
    spec = pl.BlockSpec((1, length, LANES), lambda b, j: (b, 0, j))
    o, lse = pl.pallas_call(
        functools.partial(_dil_kernel, length=length, side=side, qb=qb),
        grid=(batch, width // LANES),
        in_specs=[spec, spec, spec],
        out_specs=[spec, spec],
        out_shape=[jax.ShapeDtypeStruct(shp, F32)] * 2,
        compiler_params=_cparams(2),
        name=f"dil_attn_d{dilation}",
    )(q.reshape(shp), k.reshape(shp), v.reshape(shp))
    return o.reshape(q.shape), lse.reshape(q.shape)


def _merge_kernel(x_ref, ona_ref, o1_ref, o2_ref, o3_ref, l1_ref, l2_ref, l3_ref, sgn_ref, sgd_ref,
                  wna_ref, wdil_ref, wout_ref, g_ref, x1_ref, h2_ref, *relayout_scratch):
    scratch = iter(relayout_scratch)

    def token_order(ref, dilation):
        return ref[...] if dilation == 1 else _from_residue_layout(ref[...], next(scratch), dilation)

    o1, o2, o3 = (token_order(r, d) for r, (_, d) in zip((o1_ref, o2_ref, o3_ref), DIL_CONFIGS))
    l1, l2, l3 = (token_order(r, d) for r, (_, d) in zip((l1_ref, l2_ref, l3_ref), DIL_CONFIGS))
    m = jnp.maximum(jnp.maximum(l1, l2), l3)
    w1, w2, w3 = jnp.exp(l1 - m), jnp.exp(l2 - m), jnp.exp(l3 - m)
    od = (w1 * o1 + w2 * o2 + w3 * o3) / (w1 + w2 + w3)
    merged = (sgn_ref[...].astype(F32) * _dot(ona_ref[...], wna_ref[...])
              + sgd_ref[...].astype(F32) * _dot(od.astype(BF16), wdil_ref[...]))
    x1 = x_ref[...] + _dot(merged.astype(BF16), wout_ref[...])
    x1_ref[...] = x1
    h2_ref[...] = _rms(x1, g_ref[...])


def _merge(x2d, ona, os_, ls_, sgn, sgd, wna, wdil, wout, gain, tm=256):
    T, D = x2d.shape
    row = lambda w: pl.BlockSpec((tm, w), lambda i: (i, 0))
    residue = [pl.BlockSpec((tm // d, d * DIL_GROUP_WIDTH), lambda i: (i, 0)) for _, d in DIL_CONFIGS]
    n_relayout = 2 * sum(d > 1 for _, d in DIL_CONFIGS)
    return pl.pallas_call(
        _merge_kernel,
        grid=(T // tm,),
        in_specs=[row(D), row(ona.shape[1])] + residue * 2 + [row(D), row(D),
                  _const_spec(wna.shape), _const_spec(wdil.shape), _const_spec(wout.shape), _const_spec((1, D))],
        out_specs=[row(D), row(D)],
        out_shape=[jax.ShapeDtypeStruct((T, D), F32)] * 2,
        scratch_shapes=[_relayout_scratch(tm)] * n_relayout,
        compiler_params=_cparams(1),
        name="merge",
    )(x2d, ona, *os_, *ls_, sgn, sgd, wna, wdil, wout, gain)


def _topk_rows(s, k, payload=None):
    n = s.shape[0]
    row = lax.broadcasted_iota(jnp.int32, s.shape, 0)
    vals, idxs = [], []
    for _ in range(k):
        m = jnp.max(s, axis=0, keepdims=True)
        idx = jnp.min(jnp.where(s == m, row, n), axis=0, keepdims=True)
        sel = row == idx
        vals.append(m)
        idxs.append(idx if payload is None else jnp.max(jnp.where(sel, payload, -1), axis=0, keepdims=True))
        s = jnp.where(sel, -jnp.inf, s)
    return jnp.concatenate(vals, axis=0), jnp.concatenate(idxs, axis=0)


def _peer_topk_kernel(h_ref, wq_ref, k1_ref, k2_ref, e_ref, g_ref):
    q = _dot(h_ref[...].astype(BF16), wq_ref[...])
    half = PEER_QDIM // 2
    s1 = _dot_nt(k1_ref[...], q[:, :half].astype(BF16))
    s2 = _dot_nt(k2_ref[...], q[:, half:].astype(BF16))
    v1, i1 = _topk_rows(s1, PEER_TOPK)
    v2, i2 = _topk_rows(s2, PEER_TOPK)
    keep = [PEER_TOPK // (i + 1) for i in range(PEER_TOPK)]
    pad = -sum(keep) % SUBLANES
    tm = v1.shape[1]
    cand = jnp.concatenate([v1[i:i + 1] + v2[:n] for i, n in enumerate(keep)]
                           + [jnp.full((pad, tm), -jnp.inf, F32)], axis=0)
    cidx = jnp.concatenate([i1[i:i + 1] * PEER_NKEYS + i2[:n] for i, n in enumerate(keep)]
                           + [jnp.full((pad, tm), -1, jnp.int32)], axis=0)
    sc, e = _topk_rows(cand, PEER_TOPK, payload=cidx)
    p = jnp.exp(sc - sc[0:1])
    e_ref[...] = e
    g_ref[...] = p / jnp.sum(p, axis=0, keepdims=True)


def _peer_topk(h2, wq, k1, k2, tm=512):
    T, D = h2.shape
    out = pl.BlockSpec((PEER_TOPK, tm), lambda i, h: (h, i))
    return pl.pallas_call(
        _peer_topk_kernel,
        grid=(T // tm, PEER_HEADS),
        in_specs=[pl.BlockSpec((tm, D), lambda i, h: (i, 0)), pl.BlockSpec((D, PEER_QDIM), lambda i, h: (0, h)),
                  _const_spec(k1.shape), _const_spec(k2.shape)],
        out_specs=[out, out],
        out_shape=[jax.ShapeDtypeStruct((PEER_SLOTS, T), jnp.int32), jax.ShapeDtypeStruct((PEER_SLOTS, T), F32)],
        compiler_params=_cparams(2),
        name="peer_topk",
    )(h2, wq, k1, k2)


WORD_ROWS = 4
GATHER_PITCH = 136
HALF = 512


def _pack_table(w):
    e, d = w.shape
    wb = w.astype(BF16)
    pairs = jnp.stack([wb[:, :HALF], wb[:, HALF:]], axis=-1)
    return lax.bitcast_convert_type(pairs, jnp.uint32).reshape(e * WORD_ROWS, LANES)


INDEX_LOOKAHEAD = 8


def _gather_rows(idx_ref, t, tab_ref, gbuf):
    starts = [idx_ref[t, j] for j in range(INDEX_LOOKAHEAD)]
    for j in range(PEER_SLOTS):
        if j + INDEX_LOOKAHEAD < PEER_SLOTS:
            starts.append(idx_ref[t, j + INDEX_LOOKAHEAD])
        row0 = pl.multiple_of(starts[j], WORD_ROWS)
        gbuf[pl.ds(j, WORD_ROWS, stride=GATHER_PITCH), :] = tab_ref[pl.ds(row0, WORD_ROWS), :]


def _index_copy(idx_hbm, idx_ref, sem, step):
    tb = idx_ref.shape[0]
    return pltpu.make_async_copy(idx_hbm.at[pl.ds(step * tb, tb)], idx_ref, sem)


def _await_indices(idx_hbm, idx_ref, sem):
    step = pl.program_id(0)

    @pl.when(step == 0)
    def _():
        _index_copy(idx_hbm, idx_ref, sem, step).start()

    _index_copy(idx_hbm, idx_ref, sem, step).wait()


def _request_next_indices(idx_hbm, idx_ref, sem):
    step = pl.program_id(0)

    @pl.when(step + 1 < pl.num_programs(0))
    def _():
        _index_copy(idx_hbm, idx_ref, sem, step + 1).start()


def _gathered_matrix(gbuf):
    planes = [pltpu.bitcast(gbuf[i * GATHER_PITCH:i * GATHER_PITCH + PEER_SLOTS, :], BF16) for i in range(WORD_ROWS)]
    return jnp.concatenate(planes, axis=1)


def _lhs16(rows, width):
    r = lax.broadcasted_iota(jnp.int32, (16, width), 0)
    out = jnp.zeros((16, width), F32)
    for k, v in enumerate(rows):
        out = jnp.where(r == k, v, out)
    return out.astype(BF16)


PEER_TOKENS_PER_STEP = 64
PIPELINE_LAG = 2


def _pipelined_tokens(tb, gather, compute, bufs):
    n = len(bufs)
    for t in range(tb + PIPELINE_LAG):
        if t < tb:
            gather(t, bufs[t % n])
        if t >= PIPELINE_LAG:
            compute(t - PIPELINE_LAG, bufs[(t - PIPELINE_LAG) % n])


def _peer_u_kernel(idx_hbm, h_ref, tab_ref, g_ref, act_ref, hhi_ref, hlo_ref, c_ref, ga0, ga1, gb0, gb1,
                   idx_ref, idx_sem, *, tb):
    h = h_ref[...]
    hi = h.astype(BF16).astype(F32)
    hhi_ref[...] = hi
    hlo_ref[...] = h - hi
    even = (lax.broadcasted_iota(jnp.int32, (SUBLANES, 2 * PEER_SLOTS), 1) % 2) == 0
    _await_indices(idx_hbm, idx_ref, idx_sem)

    def compute(t, gbuf):
        row = pl.ds(t, 1)
        lhs = _lhs16([hhi_ref[row, :HALF], hhi_ref[row, HALF:], hlo_ref[row, :HALF], hlo_ref[row, HALF:]], HALF)
        res = _dot_nt(lhs, _gathered_matrix(gbuf))[:SUBLANES]
        s = res + pltpu.roll(res, 6, 0)
        c_ref[row, :] = jnp.where(even, s, pltpu.roll(s, 7, 0))[0:1]

    _pipelined_tokens(tb, lambda t, g: _gather_rows(idx_ref, t, tab_ref, g), compute, (ga0, ga1, gb0, gb1))

    c = c_ref[...]
    a = c + pltpu.roll(c, 2 * PEER_SLOTS - 1, 1)
    gelu = 0.5 * a * (1.0 + lax.erf(a * np.float32(np.sqrt(0.5))))
    lane_even = (lax.broadcasted_iota(jnp.int32, a.shape, 1) % 2) == 0
    act_ref[...] = jnp.where(lane_even, gelu * g_ref[...], 0.0)
    _request_next_indices(idx_hbm, idx_ref, idx_sem)


def _gather_scratch(tb):
    return ([pltpu.VMEM((WORD_ROWS * GATHER_PITCH, LANES), jnp.uint32)] * 4
            + [pltpu.SMEM((tb, PEER_SLOTS), jnp.int32), pltpu.SemaphoreType.DMA(())])


def _peer_u(idx, h2, tab, gates, tb=PEER_TOKENS_PER_STEP):
    T, D = h2.shape
    wide = pl.BlockSpec((tb, 2 * PEER_SLOTS), lambda i: (i, 0))
    return pl.pallas_call(
        functools.partial(_peer_u_kernel, tb=tb),
        grid=(T // tb,),
        in_specs=[pl.BlockSpec(memory_space=pl.ANY),
                  pl.BlockSpec((tb, D), lambda i: (i, 0)), _const_spec(tab.shape), wide],
        out_specs=wide,
        out_shape=jax.ShapeDtypeStruct((T, 2 * PEER_SLOTS), F32),
        scratch_shapes=[pltpu.VMEM((tb, D), F32)] * 2 + [pltpu.VMEM((tb, 2 * PEER_SLOTS), F32)] + _gather_scratch(tb),
        compiler_params=_cparams(1),
        name="peer_u",
    )(idx, h2, tab, gates)


def _peer_v_kernel(idx_hbm, act_ref, x_ref, tab_ref, o_ref, ahi_ref, alo_ref, bhi_ref, blo_ref, acc_ref,
                   ga0, ga1, gb0, gb1, idx_ref, idx_sem, *, tb):
    a = act_ref[...]
    b = pltpu.roll(a, 1, 1)
    for src, hi_ref, lo_ref in ((a, ahi_ref, alo_ref), (b, bhi_ref, blo_ref)):
        hi = src.astype(BF16).astype(F32)
        hi_ref[...] = hi
        lo_ref[...] = src - hi
    _await_indices(idx_hbm, idx_ref, idx_sem)

    def compute(t, gbuf):
        row = pl.ds(t, 1)
        lhs = _lhs16([ahi_ref[row, :], bhi_ref[row, :], alo_ref[row, :], blo_ref[row, :]], 2 * PEER_SLOTS)
        res = _dot(lhs, _gathered_matrix(gbuf))[:SUBLANES]
        s = res + pltpu.roll(res, 6, 0)
        acc_ref[row, :HALF] = s[0:1]
        acc_ref[row, HALF:] = s[1:2]

    _pipelined_tokens(tb, lambda t, g: _gather_rows(idx_ref, t, tab_ref, g), compute, (ga0, ga1, gb0, gb1))
    o_ref[...] = x_ref[...] + acc_ref[...]
    _request_next_indices(idx_hbm, idx_ref, idx_sem)


def _peer_v(idx, act, x2d, tab, tb=PEER_TOKENS_PER_STEP):
    T, D = x2d.shape
    tok = pl.BlockSpec((tb, D), lambda i: (i, 0))
    wide = pl.BlockSpec((tb, 2 * PEER_SLOTS), lambda i: (i, 0))
    return pl.pallas_call(
        functools.partial(_peer_v_kernel, tb=tb),
        grid=(T // tb,),
        in_specs=[pl.BlockSpec(memory_space=pl.ANY), wide, tok, _const_spec(tab.shape)],
        out_specs=tok,
        out_shape=jax.ShapeDtypeStruct((T, D), F32),
        scratch_shapes=[pltpu.VMEM((tb, 2 * PEER_SLOTS), F32)] * 4 + [pltpu.VMEM((tb, D), F32)] + _gather_scratch(tb),
        compiler_params=_cparams(1),
        name="peer_v",
    )(idx, act, x2d, tab)


def _ple_kernel(x_ref, p_ref, g_ref, wg_ref, wp_ref, o_ref):
    x = x_ref[...]
    h = _rms(x, g_ref[...]).astype(BF16)
    o_ref[...] = x + jax.nn.sigmoid(_dot(h, wg_ref[...])) * _dot(p_ref[...].astype(BF16), wp_ref[...])


def _ple(x2d, p2d, gain, wg, wp, tm=256):
    T, D = x2d.shape
    row = lambda w: pl.BlockSpec((tm, w), lambda i: (i, 0))
    return pl.pallas_call(
        _ple_kernel,
        grid=(T // tm,),
        in_specs=[row(D), row(p2d.shape[1]), _const_spec((1, D)), _const_spec(wg.shape), _const_spec(wp.shape)],
        out_specs=row(D),
        out_shape=jax.ShapeDtypeStruct((T, D), F32),
        compiler_params=_cparams(1),
        name="ple",
    )(x2d, p2d, gain, wg, wp)


def _rotary_tables(seq):
    half = ROT_DIM // 2
    inv_freq = jnp.power(jnp.float32(ROPE_THETA), -jnp.arange(half, dtype=F32) * 2.0 / ROT_DIM)
    ang = jnp.arange(seq).astype(F32)[:, None] * inv_freq[None, :]
    cos, sin = jnp.cos(ang), jnp.sin(ang)
    pad = HEAD_DIM - ROT_DIM
    ra = jnp.concatenate([cos, cos, jnp.ones((seq, pad), F32)], axis=1)
    rb = jnp.concatenate([-sin, sin, jnp.zeros((seq, pad), F32)], axis=1)
    return jnp.tile(ra, (1, DIL_HEADS_PER_GROUP)), jnp.tile(rb, (1, DIL_HEADS_PER_GROUP))


def kernel(x, p, norm_mix, w_in, qk_norm_na, na_rel_bias, qk_norm_dil, w_branch_na, w_branch_dil, w_out, norm_ffn,
           peer_w_query, peer_sub_keys, peer_expert_u, peer_expert_v, norm_ple, w_ple_gate, w_ple):
    B, S, D = x.shape
    T = B * S
    depth = w_in.shape[0]
    wna = NA_HEADS * HEAD_DIM
    wdil = len(DIL_CONFIGS) * DIL_GROUP_WIDTH
    rot_a, rot_b = _rotary_tables(S)
    x2d = x.reshape(T, D)
    for i in range(depth):
        wi = w_in[i].astype(BF16)
        nna = jnp.tile(qk_norm_na[i], (1, NA_HEADS))
        ndil = jnp.tile(qk_norm_dil[i], (1, DIL_HEADS_PER_GROUP))
        (qa, ka, va, q1, k1, v1, q2, k2, v2, q3, k3, v3, sgn, sgd) = _in_proj(
            x2d, norm_mix[i][None], wi[:, :3 * wna], wi[:, 3 * wna:3 * wna + 3 * wdil], wi[:, 3 * wna + 3 * wdil:],
            nna, ndil, rot_a, rot_b, S)

        ona = _na_attn(qa, ka, va, _na_bias_table(na_rel_bias[i]), B, S)
        dil = [_dil_attn(q, k, v, B, S, window, dilation)
               for (q, k, v), (window, dilation) in zip(((q1, k1, v1), (q2, k2, v2), (q3, k3, v3)), DIL_CONFIGS)]

        x1, h2 = _merge(x2d, ona, [o for o, _ in dil], [l for _, l in dil], sgn, sgd,
                        w_branch_na[i].astype(BF16), w_branch_dil[i].astype(BF16), w_out[i].astype(BF16),
                        norm_ffn[i][None])

        e_t, g_t = _peer_topk(h2, peer_w_query[i].astype(BF16), peer_sub_keys[i, 0].astype(BF16),
                              peer_sub_keys[i, 1].astype(BF16))
        idx = e_t.T * WORD_ROWS
        gates = jnp.stack([g_t.T, jnp.zeros((T, PEER_SLOTS), F32)], axis=-1).reshape(T, 2 * PEER_SLOTS)
        act = _peer_u(idx, h2, _pack_table(peer_expert_u[i]), gates)
        x2 = _peer_v(idx, act, x1, _pack_table(peer_expert_v[i]))

        x2d = _ple(x2, p[i].reshape(T, -1), norm_ple[i][None], w_ple_gate[i].astype(BF16), w_ple[i].astype(BF16))
    return x2d.reshape(B, S, D)
```

```python
import functools

import numpy as np
import jax
import jax.numpy as jnp
from jax import lax
from jax.experimental import pallas as pl
from jax.experimental.pallas import tpu as pltpu

F32 = jnp.float32
BF16 = jnp.bfloat16

HEAD_DIM = 64
GRID_W = 64
NA_HEADS = 8
NA_WIN_ROWS = 8
NA_WIN_COLS = 16
DIL_CONFIGS = ((128, 1), (512, 4), (2048, 16))
DIL_HEADS_PER_GROUP = 4
DIL_GROUP_WIDTH = DIL_HEADS_PER_GROUP * HEAD_DIM
ROT_DIM = HEAD_DIM // 4
ROPE_THETA = 500000.0
PEER_HEADS = 8
PEER_NKEYS = 128
PEER_QDIM = 256
PEER_TOPK = 16
PEER_SLOTS = PEER_HEADS * PEER_TOPK
RMS_EPS = 1e-6
NEG_BIG = -1e30

LANES = 128
SUBLANES = 8
VMEM_LIMIT_BYTES = 48 * 1024 * 1024


def _cparams(n_axes):
    return pltpu.CompilerParams(dimension_semantics=("arbitrary",) * n_axes, vmem_limit_bytes=VMEM_LIMIT_BYTES)


def _const_spec(shape):
    nd = len(shape)
    return pl.BlockSpec(shape, lambda *_: (0,) * nd, pipeline_mode=pl.Buffered(1))


def _rms(x, gain):
    return x * lax.rsqrt(jnp.mean(x * x, axis=-1, keepdims=True) + RMS_EPS) * gain


def _dot(a, b):
    return jnp.dot(a, b, preferred_element_type=F32)


def _dot_nt(a, b):
    return lax.dot_general(a, b, (((1,), (1,)), ((), ())), preferred_element_type=F32)


def _relayout_scratch(tm):
    return pltpu.VMEM((DIL_GROUP_WIDTH // LANES, tm, LANES), F32)


def _to_residue_layout(x, scr_ref, d):
    tm = x.shape[0]
    slabs = DIL_GROUP_WIDTH // LANES
    for c in range(slabs):
        scr_ref[c] = x[:, c * LANES:(c + 1) * LANES]
    return jnp.concatenate([scr_ref[c, pl.ds(r, tm // d, stride=d), :] for r in range(d) for c in range(slabs)], axis=1)


def _from_residue_layout(y, scr_ref, d):
    rows = y.shape[0]
    slabs = DIL_GROUP_WIDTH // LANES
    for r in range(d):
        for c in range(slabs):
            lo = r * DIL_GROUP_WIDTH + c * LANES
            scr_ref[c, pl.ds(r, rows, stride=d), :] = y[:, lo:lo + LANES]
    return jnp.concatenate([scr_ref[c] for c in range(slabs)], axis=1)


def _head_rms(q, bd_ref, gain):
    outs = []
    for c in range(q.shape[1] // 256):
        qc = q[:, c * 256:(c + 1) * 256]
        sq = qc * qc
        hi = sq.astype(BF16)
        lo = (sq - hi.astype(F32)).astype(BF16)
        ssq = _dot(hi, bd_ref[...]) + _dot(lo, bd_ref[...])
        outs.append(qc * lax.rsqrt(ssq * (1.0 / HEAD_DIM) + RMS_EPS))
    return jnp.concatenate(outs, axis=1) * gain


def _rotary(q, ra, rb):
    lane = lax.broadcasted_iota(jnp.int32, q.shape, 1) % HEAD_DIM
    partner = jnp.where(lane < ROT_DIM // 2, pltpu.roll(q, 256 - ROT_DIM // 2, 1), pltpu.roll(q, ROT_DIM // 2, 1))
    return q * ra + partner * rb


def _in_proj_kernel(x_ref, g_ref, wna_ref, wdil_ref, wgate_ref, nna_ref, ndil_ref, ra_ref, rb_ref, bd_ref,
                    qa_ref, ka_ref, va_ref,
                    q1_ref, k1_ref, v1_ref, q2_ref, k2_ref, v2_ref, q3_ref, k3_ref, v3_ref,
                    sgn_ref, sgd_ref, *relayout_scratch):
    scratch = iter(relayout_scratch)
    h = _rms(x_ref[...], g_ref[...]).astype(BF16)
    scale = HEAD_DIM ** -0.5

    na = _dot(h, wna_ref[...])
    wna = NA_HEADS * HEAD_DIM
    qa_ref[...] = (_head_rms(na[:, :wna], bd_ref, nna_ref[0:1, :]) * scale).astype(BF16)
    ka_ref[...] = _head_rms(na[:, wna:2 * wna], bd_ref, nna_ref[1:2, :]).astype(BF16)
    va_ref[...] = na[:, 2 * wna:].astype(BF16)

    dil = _dot(h, wdil_ref[...])
    wd = len(DIL_CONFIGS) * DIL_GROUP_WIDTH
    ra = ra_ref[...]
    rb = rb_ref[...]
    q_refs = (q1_ref, q2_ref, q3_ref)
    k_refs = (k1_ref, k2_ref, k3_ref)
    v_refs = (v1_ref, v2_ref, v3_ref)
    for g in range(len(DIL_CONFIGS)):
        lo = g * DIL_GROUP_WIDTH
        hi = lo + DIL_GROUP_WIDTH
        q = _head_rms(dil[:, lo:hi], bd_ref, ndil_ref[0:1, :])
        k = _head_rms(dil[:, wd + lo:wd + hi], bd_ref, ndil_ref[1:2, :])
        dilation = DIL_CONFIGS[g][1]
        for ref, val in ((q_refs[g], _rotary(q, ra, rb) * scale), (k_refs[g], _rotary(k, ra, rb)),
                         (v_refs[g], dil[:, 2 * wd + lo:2 * wd + hi])):
            if dilation > 1:
                val = _to_residue_layout(val, next(scratch), dilation)
            ref[...] = val.astype(BF16)

    gate = _dot(h, wgate_ref[...])
    d = sgn_ref.shape[1]
    sgn_ref[...] = jax.nn.sigmoid(gate[:, :d]).astype(BF16)
    sgd_ref[...] = jax.nn.sigmoid(gate[:, d:]).astype(BF16)


def _in_proj(x2d, gain, w_na, w_dil, w_gate, nna, ndil, rot_a, rot_b, seq, tm=256):
    T, D = x2d.shape
    bd = jnp.asarray(np.kron(np.eye(256 // HEAD_DIM), np.ones((HEAD_DIM, HEAD_DIM))), BF16)
    wna = NA_HEADS * HEAD_DIM
    nseq = seq // tm
    row = lambda w: pl.BlockSpec((tm, w), lambda i: (i, 0))
    rot = pl.BlockSpec((tm, DIL_GROUP_WIDTH), lambda i: (i % nseq, 0))
    dil_shapes = [(T // d, d * DIL_GROUP_WIDTH) for _, d in DIL_CONFIGS for _ in range(3)]
    dil_specs = [pl.BlockSpec((tm // d, d * DIL_GROUP_WIDTH), lambda i: (i, 0)) for _, d in DIL_CONFIGS for _ in range(3)]
    outs = [jax.ShapeDtypeStruct((T, wna), BF16)] * 3 + [jax.ShapeDtypeStruct(s, BF16) for s in dil_shapes] \
        + [jax.ShapeDtypeStruct((T, D), BF16)] * 2
    n_relayout = 3 * sum(d > 1 for _, d in DIL_CONFIGS)
    return pl.pallas_call(
        _in_proj_kernel,
        grid=(T // tm,),
        in_specs=[row(D), _const_spec((1, D)), _const_spec(w_na.shape), _const_spec(w_dil.shape),
                  _const_spec(w_gate.shape), _const_spec(nna.shape), _const_spec(ndil.shape), rot, rot,
                  _const_spec(bd.shape)],
        out_specs=[row(wna)] * 3 + dil_specs + [row(D)] * 2,
        out_shape=outs,
        scratch_shapes=[_relayout_scratch(tm)] * n_relayout,
        compiler_params=_cparams(1),
        name="in_proj",
    )(x2d, gain, w_na, w_dil, w_gate, nna, ndil, rot_a, rot_b, bd)


def _na_row_start(r, rows):
    return jnp.clip(r - NA_WIN_ROWS // 2, 0, rows - NA_WIN_ROWS)


NA_ROWS_PER_STEP = 2


def _head_pair_rows(x, first):
    zero = jnp.zeros_like(x)
    return jnp.concatenate([jnp.where(first, x, zero), jnp.where(first, zero, x)], axis=0)


def _na_kernel(q_ref, k_ref, v_ref, *rest, rows):
    b_refs, o_ref = rest[:NA_ROWS_PER_STEP], rest[NA_ROWS_PER_STEP]
    nk = NA_WIN_ROWS * GRID_W
    pair = 2 * HEAD_DIM
    first = lax.broadcasted_iota(jnp.int32, (GRID_W, pair), 1) < HEAD_DIM
    for sub, b_ref in enumerate(b_refs):
        r = pl.program_id(1) * NA_ROWS_PER_STEP + sub
        off = pl.multiple_of(_na_row_start(r, rows) * GRID_W, GRID_W)
        kw = k_ref[pl.ds(off, nk), :]
        vw = v_ref[pl.ds(off, nk), :]
        q = q_ref[sub * GRID_W:(sub + 1) * GRID_W, :]
        outs = []
        for hp in range(NA_HEADS // 2):
            sl = slice(hp * pair, (hp + 1) * pair)
            s = _dot_nt(_head_pair_rows(q[:, sl], first), kw[:, sl]) + b_ref[0, hp]
            m = jnp.max(s, axis=-1, keepdims=True)
            p = jnp.exp(s - m)
            l = jnp.sum(p, axis=-1, keepdims=True)
            o = _dot(p.astype(BF16), vw[:, sl]) / l
            outs.append(jnp.where(first, o[:GRID_W], o[GRID_W:]))
        o_ref[sub * GRID_W:(sub + 1) * GRID_W, :] = jnp.concatenate(outs, axis=1).astype(BF16)


def _na_bias_table(rpb):
    qc = np.arange(GRID_W)[:, None]
    kc = np.arange(GRID_W)[None, :]
    cs = np.clip(qc - NA_WIN_COLS // 2, 0, GRID_W - NA_WIN_COLS)
    mask = (kc >= cs) & (kc < cs + NA_WIN_COLS)
    dc = np.clip(kc - qc + NA_WIN_COLS - 1, 0, 2 * NA_WIN_COLS - 2)
    colb = jnp.where(mask[None, None], rpb.astype(F32)[:, :, dc], NEG_BIG)
    H = rpb.shape[0]
    tabs = [colb[:, s:s + NA_WIN_ROWS].transpose(0, 2, 1, 3).reshape(H // 2, 2 * GRID_W, NA_WIN_ROWS * GRID_W)
            for s in range(NA_WIN_ROWS)]
    return jnp.stack(tabs)


def _na_attn(qa, ka, va, bias_tab, batch, seq):
    T, W = qa.shape
    rows = seq // GRID_W
    steps = rows // NA_ROWS_PER_STEP

    def bias_spec(sub):
        def idx(b, s):
            r = s * NA_ROWS_PER_STEP + sub
            return (_na_row_start(r, rows) - r + NA_WIN_ROWS - 1, 0, 0, 0)
        return pl.BlockSpec((1,) + bias_tab.shape[1:], idx)

    kv = pl.BlockSpec((seq, W), lambda b, s: (b, 0))
    qo = pl.BlockSpec((NA_ROWS_PER_STEP * GRID_W, W), lambda b, s: (b * steps + s, 0))
    return pl.pallas_call(
        functools.partial(_na_kernel, rows=rows),
        grid=(batch, steps),
        in_specs=[qo, kv, kv] + [bias_spec(sub) for sub in range(NA_ROWS_PER_STEP)],
        out_specs=qo,
        out_shape=jax.ShapeDtypeStruct((T, W), BF16),
        compiler_params=_cparams(2),
        name="na_attn",
    )(qa, ka, va, *([bias_tab] * NA_ROWS_PER_STEP))


DIL_QUERY_BLOCK = 128


def _dil_kernel(q_ref, k_ref, v_ref, o_ref, lse_ref, *, length, side, qb):
    kb = qb + 2 * side
    nblk = length // qb
    qi = lax.broadcasted_iota(jnp.int32, (2 * qb, kb), 0) % qb
    kj = lax.broadcasted_iota(jnp.int32, (2 * qb, kb), 1)
    rel = kj - qi
    first = lax.broadcasted_iota(jnp.int32, (qb, LANES), 1) < HEAD_DIM

    def block(i, _):
        qs = pl.multiple_of(i * qb, qb)
        ws = pl.multiple_of(jnp.clip(qs - side, 0, length - kb), side)
        delta = rel + (ws - qs)
        band = (delta >= -side) & (delta <= side)
        q = q_ref[0, pl.ds(qs, qb), :]
        k = k_ref[0, pl.ds(ws, kb), :]
        v = v_ref[0, pl.ds(ws, kb), :]
        s = jnp.where(band, _dot_nt(_head_pair_rows(q, first), k), NEG_BIG)
        m = jnp.max(s, axis=-1, keepdims=True)
        p = jnp.exp(s - m)
        l = jnp.sum(p, axis=-1, keepdims=True)
        o = _dot(p.astype(BF16), v) / l
        lse = jnp.broadcast_to(m + jnp.log(l), (2 * qb, LANES))
        o_ref[0, pl.ds(qs, qb), :] = jnp.where(first, o[:qb], o[qb:])
        lse_ref[0, pl.ds(qs, qb), :] = jnp.where(first, lse[:qb], lse[qb:])
        return 0

    lax.fori_loop(0, nblk, block, 0)


def _dil_attn(q, k, v, batch, seq, window, dilation):
    length = seq // dilation
    side = window // (2 * dilation)
    qb = min(DIL_QUERY_BLOCK, length - 2 * side)
    width = dilation * DIL_GROUP_WIDTH
    shp = (batch, length, width)
    spec = pl.BlockSpec((1, length, LANES), lambda b, j: (b, 0, j))
    o, lse = pl.pallas_call(
        functools.partial(_dil_kernel, length=length, side=side, qb=qb),
        grid=(batch, width // LANES),
        in_specs=[spec, spec, spec],
        out_specs=[spec, spec],
        out_shape=[jax.ShapeDtypeStruct(shp, F32)] * 2,
        compiler_params=_cparams(2),
        name=f"dil_attn_d{dilation}",
    )(q.reshape(shp), k.reshape(shp), v.reshape(shp))
    return o.reshape(q.shape), lse.reshape(q.shape)


def _merge_kernel(x_ref, ona_ref, o1_ref, o2_ref, o3_ref, l1_ref, l2_ref, l3_ref, sgn_ref, sgd_ref,
                  wna_ref, wdil_ref, wout_ref, g_ref, x1_ref, h2_ref, *relayout_scratch):
    scratch = iter(relayout_scratch)

    def token_order(ref, dilation):
        return ref[...] if dilation == 1 else _from_residue_layout(ref[...], next(scratch), dilation)

    o1, o2, o3 = (token_order(r, d) for r, (_, d) in zip((o1_ref, o2_ref, o3_ref), DIL_CONFIGS))
    l1, l2, l3 = (token_order(r, d) for r, (_, d) in zip((l1_ref, l2_ref, l3_ref), DIL_CONFIGS))
    m = jnp.maximum(jnp.maximum(l1, l2), l3)
    w1, w2, w3 = jnp.exp(l1 - m), jnp.exp(l2 - m), jnp.exp(l3 - m)
    od = (w1 * o1 + w2 * o2 + w3 * o3) / (w1 + w2 + w3)
    merged = (sgn_ref[...].astype(F32) * _dot(ona_ref[...], wna_ref[...])
              + sgd_ref[...].astype(F32) * _dot(od.astype(BF16), wdil_ref[...]))
    x1 = x_ref[...] + _dot(merged.astype(BF16), wout_ref[...])
    x1_ref[...] = x1
    h2_ref[...] = _rms(x1, g_ref[...])


def _merge(x2d, ona, os_, ls_, sgn, sgd, wna, wdil, wout, gain, tm=256):
    T, D = x2d.shape
    row = lambda w: pl.BlockSpec((tm, w), lambda i: (i, 0))
    residue = [pl.BlockSpec((tm // d, d * DIL_GROUP_WIDTH), lambda i: (i, 0)) for _, d in DIL_CONFIGS]
    n_relayout = 2 * sum(d > 1 for _, d in DIL_CONFIGS)
    return pl.pallas_call(
        _merge_kernel,
        grid=(T // tm,),
        in_specs=[row(D), row(ona.shape[1])] + residue * 2 + [row(D), row(D),
                  _const_spec(wna.shape), _const_spec(wdil.shape), _const_spec(wout.shape), _const_spec((1, D))],
        out_specs=[row(D), row(D)],
        out_shape=[jax.ShapeDtypeStruct((T, D), F32)] * 2,
        scratch_shapes=[_relayout_scratch(tm)] * n_relayout,
        compiler_params=_cparams(1),
        name="merge",
    )(x2d, ona, *os_, *ls_, sgn, sgd, wna, wdil, wout, gain)


def _topk_rows(s, k, payload=None):
    n = s.shape[0]
    row = lax.broadcasted_iota(jnp.int32, s.shape, 0)
    vals, idxs = [], []
    for _ in range(k):
        m = jnp.max(s, axis=0, keepdims=True)
        idx = jnp.min(jnp.where(s == m, row, n), axis=0, keepdims=True)
        sel = row == idx
        vals.append(m)
        idxs.append(idx if payload is None else jnp.max(jnp.where(sel, payload, -1), axis=0, keepdims=True))
        s = jnp.where(sel, -jnp.inf, s)
    return jnp.concatenate(vals, axis=0), jnp.concatenate(idxs, axis=0)


def _peer_topk_kernel(h_ref, wq_ref, k1_ref, k2_ref, e_ref, g_ref):
    q = _dot(h_ref[...].astype(BF16), wq_ref[...])
    half = PEER_QDIM // 2
    s1 = _dot_nt(k1_ref[...], q[:, :half].astype(BF16))
    s2 = _dot_nt(k2_ref[...], q[:, half:].astype(BF16))
    v1, i1 = _topk_rows(s1, PEER_TOPK)
    v2, i2 = _topk_rows(s2, PEER_TOPK)
    keep = [PEER_TOPK // (i + 1) for i in range(PEER_TOPK)]
    pad = -sum(keep) % SUBLANES
    tm = v1.shape[1]
    cand = jnp.concatenate([v1[i:i + 1] + v2[:n] for i, n in enumerate(keep)]
                           + [jnp.full((pad, tm), -jnp.inf, F32)], axis=0)
    cidx = jnp.concatenate([i1[i:i + 1] * PEER_NKEYS + i2[:n] for i, n in enumerate(keep)]
                           + [jnp.full((pad, tm), -1, jnp.int32)], axis=0)
    sc, e = _topk_rows(cand, PEER_TOPK, payload=cidx)
    p = jnp.exp(sc - sc[0:1])
    e_ref[...] = e
    g_ref[...] = p / jnp.sum(p, axis=0, keepdims=True)


def _peer_topk(h2, wq, k1, k2, tm=512):
    T, D = h2.shape
    out = pl.BlockSpec((PEER_TOPK, tm), lambda i, h: (h, i))
    return pl.pallas_call(
        _peer_topk_kernel,
        grid=(T // tm, PEER_HEADS),
        in_specs=[pl.BlockSpec((tm, D), lambda i, h: (i, 0)), pl.BlockSpec((D, PEER_QDIM), lambda i, h: (0, h)),
                  _const_spec(k1.shape), _const_spec(k2.shape)],
        out_specs=[out, out],
        out_shape=[jax.ShapeDtypeStruct((PEER_SLOTS, T), jnp.int32), jax.ShapeDtypeStruct((PEER_SLOTS, T), F32)],
        compiler_params=_cparams(2),
        name="peer_topk",
    )(h2, wq, k1, k2)


WORD_ROWS = 4
GATHER_PITCH = 136
HALF = 512


def _pack_table(w):
    e, d = w.shape
    wb = w.astype(BF16)
    pairs = jnp.stack([wb[:, :HALF], wb[:, HALF:]], axis=-1)
    return lax.bitcast_convert_type(pairs, jnp.uint32).reshape(e * WORD_ROWS, LANES)


INDEX_LOOKAHEAD = 8


def _gather_rows(idx_ref, t, tab_ref, gbuf, first_row=0, pitch=GATHER_PITCH):
    starts = [idx_ref[t, j] for j in range(INDEX_LOOKAHEAD)]
    for j in range(PEER_SLOTS):
        if j + INDEX_LOOKAHEAD < PEER_SLOTS:
            starts.append(idx_ref[t, j + INDEX_LOOKAHEAD])
        row0 = pl.multiple_of(starts[j], WORD_ROWS)
        gbuf[pl.ds(first_row + j, WORD_ROWS, stride=pitch), :] = tab_ref[pl.ds(row0, WORD_ROWS), :]


class _StagedIndices:
    def __init__(self, idx_hbm, bufs, sems):
        self.hbm, self.bufs, self.sems = idx_hbm, bufs, sems
        self.half = bufs[0].shape[0]
        self.step = pl.program_id(0)
        self.last = pl.num_programs(0) - 1

        @pl.when(self.step == 0)
        def _():
            for k in range(2):
                self._copy(self.step, k).start()

    def _copy(self, step, k):
        rows = pl.ds((2 * step + k) * self.half, self.half)
        return pltpu.make_async_copy(self.hbm.at[rows], self.bufs[k], self.sems.at[k])

    def gather(self, t, tab_ref, gbuf):
        k, local = divmod(t, self.half)
        if local == 0:
            self._copy(self.step, k).wait()
        _gather_rows(self.bufs[k], local, tab_ref, gbuf)
        if local == self.half - 1:
            self._copy(jnp.minimum(self.step + 1, self.last), k).start()

    def finish(self):
        @pl.when(self.step == self.last)
        def _():
            for k in range(2):
                self._copy(self.last, k).wait()


def _gathered_matrix(gbuf):
    planes = [pltpu.bitcast(gbuf[i * GATHER_PITCH:i * GATHER_PITCH + PEER_SLOTS, :], BF16) for i in range(WORD_ROWS)]
    return jnp.concatenate(planes, axis=1)


def _lhs16(rows, width):
    r = lax.broadcasted_iota(jnp.int32, (16, width), 0)
    out = jnp.zeros((16, width), F32)
    for k, v in enumerate(rows):
        out = jnp.where(r == k, v, out)
    return out.astype(BF16)


PEER_TOKENS_PER_STEP = 64
PIPELINE_LAG = 2


def _pipelined_tokens(tb, gather, compute, bufs):
    n = len(bufs)
    for t in range(tb + PIPELINE_LAG):
        if t < tb:
            gather(t, bufs[t % n])
        if t >= PIPELINE_LAG:
            compute(t - PIPELINE_LAG, bufs[(t - PIPELINE_LAG) % n])


def _peer_u_kernel(idx_hbm, h_ref, tab_ref, g_ref, act_ref, hhi_ref, hlo_ref, c_ref, ga0, ga1, gb0, gb1,
                   idx_a, idx_b, idx_sems, *, tb):
    h = h_ref[...]
    hi = h.astype(BF16).astype(F32)
    hhi_ref[...] = hi
    hlo_ref[...] = h - hi
    even = (lax.broadcasted_iota(jnp.int32, (SUBLANES, 2 * PEER_SLOTS), 1) % 2) == 0
    indices = _StagedIndices(idx_hbm, (idx_a, idx_b), idx_sems)

    def compute(t, gbuf):
        row = pl.ds(t, 1)
        lhs = _lhs16([hhi_ref[row, :HALF], hhi_ref[row, HALF:], hlo_ref[row, :HALF], hlo_ref[row, HALF:]], HALF)
        res = _dot_nt(lhs, _gathered_matrix(gbuf))[:SUBLANES]
        s = res + pltpu.roll(res, 6, 0)
        c_ref[row, :] = jnp.where(even, s, pltpu.roll(s, 7, 0))[0:1]

    _pipelined_tokens(tb, lambda t, g: indices.gather(t, tab_ref, g), compute, (ga0, ga1, gb0, gb1))

    c = c_ref[...]
    a = c + pltpu.roll(c, 2 * PEER_SLOTS - 1, 1)
    gelu = 0.5 * a * (1.0 + lax.erf(a * np.float32(np.sqrt(0.5))))
    lane_even = (lax.broadcasted_iota(jnp.int32, a.shape, 1) % 2) == 0
    act_ref[...] = jnp.where(lane_even, gelu * g_ref[...], 0.0)
    indices.finish()


def _gather_scratch(tb):
    return ([pltpu.VMEM((WORD_ROWS * GATHER_PITCH, LANES), jnp.uint32)] * 4
            + [pltpu.SMEM((tb // 2, PEER_SLOTS), jnp.int32)] * 2 + [pltpu.SemaphoreType.DMA((2,))])


def _peer_u(idx, h2, tab, gates, tb=PEER_TOKENS_PER_STEP):
    T, D = h2.shape
    wide = pl.BlockSpec((tb, 2 * PEER_SLOTS), lambda i: (i, 0))
    return pl.pallas_call(
        functools.partial(_peer_u_kernel, tb=tb),
        grid=(T // tb,),
        in_specs=[pl.BlockSpec(memory_space=pl.ANY),
                  pl.BlockSpec((tb, D), lambda i: (i, 0)), _const_spec(tab.shape), wide],
        out_specs=wide,
        out_shape=jax.ShapeDtypeStruct((T, 2 * PEER_SLOTS), F32),
        scratch_shapes=[pltpu.VMEM((tb, D), F32)] * 2 + [pltpu.VMEM((tb, 2 * PEER_SLOTS), F32)] + _gather_scratch(tb),
        compiler_params=_cparams(1),
        name="peer_u",
    )(idx, h2, tab, gates)


def _peer_v_kernel(idx_hbm, act_ref, x_ref, tab_ref, o_ref, ahi_ref, alo_ref, bhi_ref, blo_ref, acc_ref,
                   ga0, ga1, gb0, gb1, idx_a, idx_b, idx_sems, *, tb):
    a = act_ref[...]
    b = pltpu.roll(a, 1, 1)
    for src, hi_ref, lo_ref in ((a, ahi_ref, alo_ref), (b, bhi_ref, blo_ref)):
        hi = src.astype(BF16).astype(F32)
        hi_ref[...] = hi
        lo_ref[...] = src - hi
    indices = _StagedIndices(idx_hbm, (idx_a, idx_b), idx_sems)

    def compute(t, gbuf):
        row = pl.ds(t, 1)
        lhs = _lhs16([ahi_ref[row, :], bhi_ref[row, :], alo_ref[row, :], blo_ref[row, :]], 2 * PEER_SLOTS)
        res = _dot(lhs, _gathered_matrix(gbuf))[:SUBLANES]
        s = res + pltpu.roll(res, 6, 0)
        acc_ref[row, :HALF] = s[0:1]
        acc_ref[row, HALF:] = s[1:2]

    _pipelined_tokens(tb, lambda t, g: indices.gather(t, tab_ref, g), compute, (ga0, ga1, gb0, gb1))
    o_ref[...] = x_ref[...] + acc_ref[...]
    indices.finish()


def _peer_v(idx, act, x2d, tab, tb=PEER_TOKENS_PER_STEP):
    T, D = x2d.shape
    tok = pl.BlockSpec((tb, D), lambda i: (i, 0))
    wide = pl.BlockSpec((tb, 2 * PEER_SLOTS), lambda i: (i, 0))
    return pl.pallas_call(
        functools.partial(_peer_v_kernel, tb=tb),
        grid=(T // tb,),
        in_specs=[pl.BlockSpec(memory_space=pl.ANY), wide, tok, _const_spec(tab.shape)],
        out_specs=tok,
        out_shape=jax.ShapeDtypeStruct((T, D), F32),
        scratch_shapes=[pltpu.VMEM((tb, 2 * PEER_SLOTS), F32)] * 4 + [pltpu.VMEM((tb, D), F32)] + _gather_scratch(tb),
        compiler_params=_cparams(1),
        name="peer_v",
    )(idx, act, x2d, tab)


def _ple_kernel(x_ref, p_ref, g_ref, wg_ref, wp_ref, o_ref):
    x = x_ref[...]
    h = _rms(x, g_ref[...]).astype(BF16)
    o_ref[...] = x + jax.nn.sigmoid(_dot(h, wg_ref[...])) * _dot(p_ref[...].astype(BF16), wp_ref[...])


def _ple(x2d, p2d, gain, wg, wp, tm=256):
    T, D = x2d.shape
    row = lambda w: pl.BlockSpec((tm, w), lambda i: (i, 0))
    return pl.pallas_call(
        _ple_kernel,
        grid=(T // tm,),
        in_specs=[row(D), row(p2d.shape[1]), _const_spec((1, D)), _const_spec(wg.shape), _const_spec(wp.shape)],
        out_specs=row(D),
        out_shape=jax.ShapeDtypeStruct((T, D), F32),
        compiler_params=_cparams(1),
        name="ple",
    )(x2d, p2d, gain, wg, wp)


def _rotary_tables(seq):
    half = ROT_DIM // 2
    inv_freq = jnp.power(jnp.float32(ROPE_THETA), -jnp.arange(half, dtype=F32) * 2.0 / ROT_DIM)
    ang = jnp.arange(seq).astype(F32)[:, None] * inv_freq[None, :]
    cos, sin = jnp.cos(ang), jnp.sin(ang)
    pad = HEAD_DIM - ROT_DIM
    ra = jnp.concatenate([cos, cos, jnp.ones((seq, pad), F32)], axis=1)
    rb = jnp.concatenate([-sin, sin, jnp.zeros((seq, pad), F32)], axis=1)
    return jnp.tile(ra, (1, DIL_HEADS_PER_GROUP)), jnp.tile(rb, (1, DIL_HEADS_PER_GROUP))


def kernel(x, p, norm_mix, w_in, qk_norm_na, na_rel_bias, qk_norm_dil, w_branch_na, w_branch_dil, w_out, norm_ffn,
           peer_w_query, peer_sub_keys, peer_expert_u, peer_expert_v, norm_ple, w_ple_gate, w_ple):
    B, S, D = x.shape
    T = B * S
    depth = w_in.shape[0]
    wna = NA_HEADS * HEAD_DIM
    wdil = len(DIL_CONFIGS) * DIL_GROUP_WIDTH
    rot_a, rot_b = _rotary_tables(S)
    x2d = x.reshape(T, D)
    for i in range(depth):
        wi = w_in[i].astype(BF16)
        nna = jnp.tile(qk_norm_na[i], (1, NA_HEADS))
        ndil = jnp.tile(qk_norm_dil[i], (1, DIL_HEADS_PER_GROUP))
        (qa, ka, va, q1, k1, v1, q2, k2, v2, q3, k3, v3, sgn, sgd) = _in_proj(
            x2d, norm_mix[i][None], wi[:, :3 * wna], wi[:, 3 * wna:3 * wna + 3 * wdil], wi[:, 3 * wna + 3 * wdil:],
            nna, ndil, rot_a, rot_b, S)

        ona = _na_attn(qa, ka, va, _na_bias_table(na_rel_bias[i]), B, S)
        dil = [_dil_attn(q, k, v, B, S, window, dilation)
               for (q, k, v), (window, dilation) in zip(((q1, k1, v1), (q2, k2, v2), (q3, k3, v3)), DIL_CONFIGS)]

        x1, h2 = _merge(x2d, ona, [o for o, _ in dil], [l for _, l in dil], sgn, sgd,
                        w_branch_na[i].astype(BF16), w_branch_dil[i].astype(BF16), w_out[i].astype(BF16),
                        norm_ffn[i][None])

        e_t, g_t = _peer_topk(h2, peer_w_query[i].astype(BF16), peer_sub_keys[i, 0].astype(BF16),
                              peer_sub_keys[i, 1].astype(BF16))
        idx = e_t.T * WORD_ROWS
        gates = jnp.stack([g_t.T, jnp.zeros((T, PEER_SLOTS), F32)], axis=-1).reshape(T, 2 * PEER_SLOTS)
        act = _peer_u(idx, h2, _pack_table(peer_expert_u[i]), gates)
        x2 = _peer_v(idx, act, x1, _pack_table(peer_expert_v[i]))

        x2d = _ple(x2, p[i].reshape(T, -1), norm_ple[i][None], w_ple_gate[i].astype(BF16), w_ple[i].astype(BF16))
    return x2d.reshape(B, S, D)
```

```python
import functools

import numpy as np
import jax
import jax.numpy as jnp
from jax import lax
from jax.experimental import pallas as pl
from jax.experimental.pallas import tpu as pltpu

F32 = jnp.float32
BF16 = jnp.bfloat16

HEAD_DIM = 64
GRID_W = 64
NA_HEADS = 8
NA_WIN_ROWS = 8
NA_WIN_COLS = 16
DIL_CONFIGS = ((128, 1), (512, 4), (2048, 16))
DIL_HEADS_PER_GROUP = 4
DIL_GROUP_WIDTH = DIL_HEADS_PER_GROUP * HEAD_DIM
ROT_DIM = HEAD_DIM // 4
ROPE_THETA = 500000.0
PEER_HEADS = 8
PEER_NKEYS = 128
PEER_QDIM = 256
PEER_TOPK = 16
PEER_SLOTS = PEER_HEADS * PEER_TOPK
RMS_EPS = 1e-6
NEG_BIG = -1e30

LANES = 128
SUBLANES = 8
VMEM_LIMIT_BYTES = 48 * 1024 * 1024


def _cparams(n_axes):
    return pltpu.CompilerParams(dimension_semantics=("arbitrary",) * n_axes, vmem_limit_bytes=VMEM_LIMIT_BYTES)


def _const_spec(shape):
    nd = len(shape)
    return pl.BlockSpec(shape, lambda *_: (0,) * nd, pipeline_mode=pl.Buffered(1))


def _rms(x, gain):
    return x * lax.rsqrt(jnp.mean(x * x, axis=-1, keepdims=True) + RMS_EPS) * gain


def _dot(a, b):
    return jnp.dot(a, b, preferred_element_type=F32)


def _dot_nt(a, b):
    return lax.dot_general(a, b, (((1,), (1,)), ((), ())), preferred_element_type=F32)


def _relayout_scratch(tm):
    return pltpu.VMEM((DIL_GROUP_WIDTH // LANES, tm, LANES), F32)


def _to_residue_layout(x, scr_ref, d):
    tm = x.shape[0]
    slabs = DIL_GROUP_WIDTH // LANES
    for c in range(slabs):
        scr_ref[c] = x[:, c * LANES:(c + 1) * LANES]
    return jnp.concatenate([scr_ref[c, pl.ds(r, tm // d, stride=d), :] for r in range(d) for c in range(slabs)], axis=1)


def _from_residue_layout(y, scr_ref, d):
    rows = y.shape[0]
    slabs = DIL_GROUP_WIDTH // LANES
    for r in range(d):
        for c in range(slabs):
            lo = r * DIL_GROUP_WIDTH + c * LANES
            scr_ref[c, pl.ds(r, rows, stride=d), :] = y[:, lo:lo + LANES]
    return jnp.concatenate([scr_ref[c] for c in range(slabs)], axis=1)


def _head_rms(q, bd_ref, gain):
    outs = []
    for c in range(q.shape[1] // 256):
        qc = q[:, c * 256:(c + 1) * 256]
        sq = qc * qc
        hi = sq.astype(BF16)
        lo = (sq - hi.astype(F32)).astype(BF16)
        ssq = _dot(hi, bd_ref[...]) + _dot(lo, bd_ref[...])
        outs.append(qc * lax.rsqrt(ssq * (1.0 / HEAD_DIM) + RMS_EPS))
    return jnp.concatenate(outs, axis=1) * gain


def _rotary(q, ra, rb):
    lane = lax.broadcasted_iota(jnp.int32, q.shape, 1) % HEAD_DIM
    partner = jnp.where(lane < ROT_DIM // 2, pltpu.roll(q, 256 - ROT_DIM // 2, 1), pltpu.roll(q, ROT_DIM // 2, 1))
    return q * ra + partner * rb


def _in_proj_kernel(x_ref, g_ref, wna_ref, wdil_ref, wgate_ref, nna_ref, ndil_ref, ra_ref, rb_ref, bd_ref,
                    qa_ref, ka_ref, va_ref,
                    q1_ref, k1_ref, v1_ref, q2_ref, k2_ref, v2_ref, q3_ref, k3_ref, v3_ref,
                    sgn_ref, sgd_ref, *relayout_scratch):
    scratch = iter(relayout_scratch)
    h = _rms(x_ref[...], g_ref[...]).astype(BF16)
    scale = HEAD_DIM ** -0.5

    na = _dot(h, wna_ref[...])
    wna = NA_HEADS * HEAD_DIM
    qa_ref[...] = (_head_rms(na[:, :wna], bd_ref, nna_ref[0:1, :]) * scale).astype(BF16)
    ka_ref[...] = _head_rms(na[:, wna:2 * wna], bd_ref, nna_ref[1:2, :]).astype(BF16)
    va_ref[...] = na[:, 2 * wna:].astype(BF16)

    dil = _dot(h, wdil_ref[...])
    wd = len(DIL_CONFIGS) * DIL_GROUP_WIDTH
    ra = ra_ref[...]
    rb = rb_ref[...]
    q_refs = (q1_ref, q2_ref, q3_ref)
    k_refs = (k1_ref, k2_ref, k3_ref)
    v_refs = (v1_ref, v2_ref, v3_ref)
    for g in range(len(DIL_CONFIGS)):
        lo = g * DIL_GROUP_WIDTH
        hi = lo + DIL_GROUP_WIDTH
        q = _head_rms(dil[:, lo:hi], bd_ref, ndil_ref[0:1, :])
        k = _head_rms(dil[:, wd + lo:wd + hi], bd_ref, ndil_ref[1:2, :])
        dilation = DIL_CONFIGS[g][1]
        for ref, val in ((q_refs[g], _rotary(q, ra, rb) * scale), (k_refs[g], _rotary(k, ra, rb)),
                         (v_refs[g], dil[:, 2 * wd + lo:2 * wd + hi])):
            if dilation > 1:
                val = _to_residue_layout(val, next(scratch), dilation)
            ref[...] = val.astype(BF16)

    gate = _dot(h, wgate_ref[...])
    d = sgn_ref.shape[1]
    sgn_ref[...] = jax.nn.sigmoid(gate[:, :d]).astype(BF16)
    sgd_ref[...] = jax.nn.sigmoid(gate[:, d:]).astype(BF16)


def _in_proj(x2d, gain, w_na, w_dil, w_gate, nna, ndil, rot_a, rot_b, seq, tm=256):
    T, D = x2d.shape
    bd = jnp.asarray(np.kron(np.eye(256 // HEAD_DIM), np.ones((HEAD_DIM, HEAD_DIM))), BF16)
    wna = NA_HEADS * HEAD_DIM
    nseq = seq // tm
    row = lambda w: pl.BlockSpec((tm, w), lambda i: (i, 0))
    rot = pl.BlockSpec((tm, DIL_GROUP_WIDTH), lambda i: (i % nseq, 0))
    dil_shapes = [(T // d, d * DIL_GROUP_WIDTH) for _, d in DIL_CONFIGS for _ in range(3)]
    dil_specs = [pl.BlockSpec((tm // d, d * DIL_GROUP_WIDTH), lambda i: (i, 0)) for _, d in DIL_CONFIGS for _ in range(3)]
    outs = [jax.ShapeDtypeStruct((T, wna), BF16)] * 3 + [jax.ShapeDtypeStruct(s, BF16) for s in dil_shapes] \
        + [jax.ShapeDtypeStruct((T, D), BF16)] * 2
    n_relayout = 3 * sum(d > 1 for _, d in DIL_CONFIGS)
    return pl.pallas_call(
        _in_proj_kernel,
        grid=(T // tm,),
        in_specs=[row(D), _const_spec((1, D)), _const_spec(w_na.shape), _const_spec(w_dil.shape),
                  _const_spec(w_gate.shape), _const_spec(nna.shape), _const_spec(ndil.shape), rot, rot,
                  _const_spec(bd.shape)],
        out_specs=[row(wna)] * 3 + dil_specs + [row(D)] * 2,
        out_shape=outs,
        scratch_shapes=[_relayout_scratch(tm)] * n_relayout,
        compiler_params=_cparams(1),
        name="in_proj",
    )(x2d, gain, w_na, w_dil, w_gate, nna, ndil, rot_a, rot_b, bd)


def _na_row_start(r, rows):
    return jnp.clip(r - NA_WIN_ROWS // 2, 0, rows - NA_WIN_ROWS)


NA_ROWS_PER_STEP = 2


def _head_pair_rows(x, first):
    zero = jnp.zeros_like(x)
    return jnp.concatenate([jnp.where(first, x, zero), jnp.where(first, zero, x)], axis=0)


def _na_kernel(q_ref, k_ref, v_ref, *rest, rows):
    b_refs, o_ref = rest[:NA_ROWS_PER_STEP], rest[NA_ROWS_PER_STEP]
    nk = NA_WIN_ROWS * GRID_W
    pair = 2 * HEAD_DIM
    first = lax.broadcasted_iota(jnp.int32, (GRID_W, pair), 1) < HEAD_DIM
    for sub, b_ref in enumerate(b_refs):
        r = pl.program_id(1) * NA_ROWS_PER_STEP + sub
        off = pl.multiple_of(_na_row_start(r, rows) * GRID_W, GRID_W)
        kw = k_ref[pl.ds(off, nk), :]
        vw = v_ref[pl.ds(off, nk), :]
        q = q_ref[sub * GRID_W:(sub + 1) * GRID_W, :]
        outs = []
        for hp in range(NA_HEADS // 2):
            sl = slice(hp * pair, (hp + 1) * pair)
            s = _dot_nt(_head_pair_rows(q[:, sl], first), kw[:, sl]) + b_ref[0, hp]
            m = jnp.max(s, axis=-1, keepdims=True)
            p = jnp.exp(s - m)
            l = jnp.sum(p, axis=-1, keepdims=True)
            o = _dot(p.astype(BF16), vw[:, sl]) / l
            outs.append(jnp.where(first, o[:GRID_W], o[GRID_W:]))
        o_ref[sub * GRID_W:(sub + 1) * GRID_W, :] = jnp.concatenate(outs, axis=1).astype(BF16)


def _na_bias_table(rpb):
    qc = np.arange(GRID_W)[:, None]
    kc = np.arange(GRID_W)[None, :]
    cs = np.clip(qc - NA_WIN_COLS // 2, 0, GRID_W - NA_WIN_COLS)
    mask = (kc >= cs) & (kc < cs + NA_WIN_COLS)
    dc = np.clip(kc - qc + NA_WIN_COLS - 1, 0, 2 * NA_WIN_COLS - 2)
    colb = jnp.where(mask[None, None], rpb.astype(F32)[:, :, dc], NEG_BIG)
    H = rpb.shape[0]
    tabs = [colb[:, s:s + NA_WIN_ROWS].transpose(0, 2, 1, 3).reshape(H // 2, 2 * GRID_W, NA_WIN_ROWS * GRID_W)
            for s in range(NA_WIN_ROWS)]
    return jnp.stack(tabs)


def _na_attn(qa, ka, va, bias_tab, batch, seq):
    T, W = qa.shape
    rows = seq // GRID_W
    steps = rows // NA_ROWS_PER_STEP

    def bias_spec(sub):
        def idx(b, s):
            r = s * NA_ROWS_PER_STEP + sub
            return (_na_row_start(r, rows) - r + NA_WIN_ROWS - 1, 0, 0, 0)
        return pl.BlockSpec((1,) + bias_tab.shape[1:], idx)

    kv = pl.BlockSpec((seq, W), lambda b, s: (b, 0))
    qo = pl.BlockSpec((NA_ROWS_PER_STEP * GRID_W, W), lambda b, s: (b * steps + s, 0))
    return pl.pallas_call(
        functools.partial(_na_kernel, rows=rows),
        grid=(batch, steps),
        in_specs=[qo, kv, kv] + [bias_spec(sub) for sub in range(NA_ROWS_PER_STEP)],
        out_specs=qo,
        out_shape=jax.ShapeDtypeStruct((T, W), BF16),
        compiler_params=_cparams(2),
        name="na_attn",
    )(qa, ka, va, *([bias_tab] * NA_ROWS_PER_STEP))


DIL_QUERY_BLOCK = 128
DIL_BLOCKS_PER_ITER = 8


def _dil_kernel(q_ref, k_ref, v_ref, o_ref, lse_ref, *, length, side, qb, tiles):
    kb = qb + 2 * side
    nblk = length // qb
    qi = lax.broadcasted_iota(jnp.int32, (2 * qb, kb), 0) % qb
    kj = lax.broadcasted_iota(jnp.int32, (2 * qb, kb), 1)
    rel = kj - qi
    first = lax.broadcasted_iota(jnp.int32, (qb, LANES), 1) < HEAD_DIM

    def block(tile, i):
        lanes = slice(tile * LANES, (tile + 1) * LANES)
        qs = pl.multiple_of(i * qb, qb)
        ws = pl.multiple_of(jnp.clip(qs - side, 0, length - kb), side)
        delta = rel + (ws - qs)
        band = (delta >= -side) & (delta <= side)
        q = q_ref[0, pl.ds(qs, qb), lanes]
        k = k_ref[0, pl.ds(ws, kb), lanes]
        v = v_ref[0, pl.ds(ws, kb), lanes]
        s = jnp.where(band, _dot_nt(_head_pair_rows(q, first), k), NEG_BIG)
        m = jnp.max(s, axis=-1, keepdims=True)
        p = jnp.exp(s - m)
        l = jnp.sum(p, axis=-1, keepdims=True)
        o = _dot(p.astype(BF16), v) / l
        lse = jnp.broadcast_to(m + jnp.log(l), (2 * qb, LANES))
        o_ref[0, pl.ds(qs, qb), lanes] = jnp.where(first, o[:qb], o[qb:])
        lse_ref[0, pl.ds(qs, qb), lanes] = jnp.where(first, lse[:qb], lse[qb:])

    per_iter = min(DIL_BLOCKS_PER_ITER, nblk)

    def several(it, _):
        for tile in range(tiles):
            for u in range(per_iter):
                block(tile, it * per_iter + u)
        return 0

    lax.fori_loop(0, nblk // per_iter, several, 0)


def _dil_attn(q, k, v, batch, seq, window, dilation):
    length = seq // dilation
    side = window // (2 * dilation)
    qb = min(DIL_QUERY_BLOCK, length - 2 * side)
    width = dilation * DIL_GROUP_WIDTH
    shp = (batch, length, width)
    tiles = max(1, DIL_BLOCKS_PER_ITER // (length // qb))
    spec = pl.BlockSpec((1, length, tiles * LANES), lambda b, j: (b, 0, j))
    o, lse = pl.pallas_call(
        functools.partial(_dil_kernel, length=length, side=side, qb=qb, tiles=tiles),
        grid=(batch, width // (tiles * LANES)),
        in_specs=[spec, spec, spec],
        out_specs=[spec, spec],
        out_shape=[jax.ShapeDtypeStruct(shp, F32)] * 2,
        compiler_params=_cparams(2),
        name=f"dil_attn_d{dilation}",
    )(q.reshape(shp), k.reshape(shp), v.reshape(shp))
    return o.reshape(q.shape), lse.reshape(q.shape)


def _merge_kernel(x_ref, ona_ref, o1_ref, o2_ref, o3_ref, l1_ref, l2_ref, l3_ref, sgn_ref, sgd_ref,
                  wna_ref, wdil_ref, wout_ref, g_ref, x1_ref, h2_ref, *relayout_scratch):
    scratch = iter(relayout_scratch)

    def token_order(ref, dilation):
        return ref[...] if dilation == 1 else _from_residue_layout(ref[...], next(scratch), dilation)

    o1, o2, o3 = (token_order(r, d) for r, (_, d) in zip((o1_ref, o2_ref, o3_ref), DIL_CONFIGS))
    l1, l2, l3 = (token_order(r, d) for r, (_, d) in zip((l1_ref, l2_ref, l3_ref), DIL_CONFIGS))
    m = jnp.maximum(jnp.maximum(l1, l2), l3)
    w1, w2, w3 = jnp.exp(l1 - m), jnp.exp(l2 - m), jnp.exp(l3 - m)
    od = (w1 * o1 + w2 * o2 + w3 * o3) / (w1 + w2 + w3)
    merged = (sgn_ref[...].astype(F32) * _dot(ona_ref[...], wna_ref[...])
              + sgd_ref[...].astype(F32) * _dot(od.astype(BF16), wdil_ref[...]))
    x1 = x_ref[...] + _dot(merged.astype(BF16), wout_ref[...])
    x1_ref[...] = x1
    h2_ref[...] = _rms(x1, g_ref[...])


def _merge(x2d, ona, os_, ls_, sgn, sgd, wna, wdil, wout, gain, tm=256):
    T, D = x2d.shape
    row = lambda w: pl.BlockSpec((tm, w), lambda i: (i, 0))
    residue = [pl.BlockSpec((tm // d, d * DIL_GROUP_WIDTH), lambda i: (i, 0)) for _, d in DIL_CONFIGS]
    n_relayout = 2 * sum(d > 1 for _, d in DIL_CONFIGS)
    return pl.pallas_call(
        _merge_kernel,
        grid=(T // tm,),
        in_specs=[row(D), row(ona.shape[1])] + residue * 2 + [row(D), row(D),
                  _const_spec(wna.shape), _const_spec(wdil.shape), _const_spec(wout.shape), _const_spec((1, D))],
        out_specs=[row(D), row(D)],
        out_shape=[jax.ShapeDtypeStruct((T, D), F32)] * 2,
        scratch_shapes=[_relayout_scratch(tm)] * n_relayout,
        compiler_params=_cparams(1),
        name="merge",
    )(x2d, ona, *os_, *ls_, sgn, sgd, wna, wdil, wout, gain)


def _topk_rows(s, k, payload=None):
    n = s.shape[0]
    row = lax.broadcasted_iota(jnp.int32, s.shape, 0)
    vals, idxs = [], []
    for _ in range(k):
        m = jnp.max(s, axis=0, keepdims=True)
        idx = jnp.min(jnp.where(s == m, row, n), axis=0, keepdims=True)
        sel = row == idx
        vals.append(m)
        idxs.append(idx if payload is None else jnp.max(jnp.where(sel, payload, -1), axis=0, keepdims=True))
        s = jnp.where(sel, -jnp.inf, s)
    return jnp.concatenate(vals, axis=0), jnp.concatenate(idxs, axis=0)


def _peer_topk_kernel(h_ref, wq_ref, k1_ref, k2_ref, e_ref, g_ref):
    q = _dot(h_ref[...].astype(BF16), wq_ref[...])
    half = PEER_QDIM // 2
    s1 = _dot_nt(k1_ref[...], q[:, :half].astype(BF16))
    s2 = _dot_nt(k2_ref[...], q[:, half:].astype(BF16))
    v1, i1 = _topk_rows(s1, PEER_TOPK)
    v2, i2 = _topk_rows(s2, PEER_TOPK)
    keep = [PEER_TOPK // (i + 1) for i in range(PEER_TOPK)]
    pad = -sum(keep) % SUBLANES
    tm = v1.shape[1]
    cand = jnp.concatenate([v1[i:i + 1] + v2[:n] for i, n in enumerate(keep)]
                           + [jnp.full((pad, tm), -jnp.inf, F32)], axis=0)
    cidx = jnp.concatenate([i1[i:i + 1] * PEER_NKEYS + i2[:n] for i, n in enumerate(keep)]
                           + [jnp.full((pad, tm), -1, jnp.int32)], axis=0)
    sc, e = _topk_rows(cand, PEER_TOPK, payload=cidx)
    p = jnp.exp(sc - sc[0:1])
    e_ref[...] = e
    g_ref[...] = p / jnp.sum(p, axis=0, keepdims=True)


def _peer_topk(h2, wq, k1, k2, tm=512):
    T, D = h2.shape
    out = pl.BlockSpec((PEER_TOPK, tm), lambda i, h: (h, i))
    return pl.pallas_call(
        _peer_topk_kernel,
        grid=(T // tm, PEER_HEADS),
        in_specs=[pl.BlockSpec((tm, D), lambda i, h: (i, 0)), pl.BlockSpec((D, PEER_QDIM), lambda i, h: (0, h)),
                  _const_spec(k1.shape), _const_spec(k2.shape)],
        out_specs=[out, out],
        out_shape=[jax.ShapeDtypeStruct((PEER_SLOTS, T), jnp.int32), jax.ShapeDtypeStruct((PEER_SLOTS, T), F32)],
        compiler_params=_cparams(2),
        name="peer_topk",
    )(h2, wq, k1, k2)


WORD_ROWS = 4
GATHER_PITCH = 136
HALF = 512


def _pack_table(w):
    e, d = w.shape
    wb = w.astype(BF16)
    pairs = jnp.stack([wb[:, :HALF], wb[:, HALF:]], axis=-1)
    return lax.bitcast_convert_type(pairs, jnp.uint32).reshape(e * WORD_ROWS, LANES)


INDEX_LOOKAHEAD = 8


def _gather_rows(idx_ref, t, tab_ref, gbuf):
    starts = [idx_ref[t, j] for j in range(INDEX_LOOKAHEAD)]
    for j in range(PEER_SLOTS):
        if j + INDEX_LOOKAHEAD < PEER_SLOTS:
            starts.append(idx_ref[t, j + INDEX_LOOKAHEAD])
        row0 = pl.multiple_of(starts[j], WORD_ROWS)
        gbuf[pl.ds(j, WORD_ROWS, stride=GATHER_PITCH), :] = tab_ref[pl.ds(row0, WORD_ROWS), :]


class _StagedIndices:
    def __init__(self, idx_hbm, bufs, sems):
        self.hbm, self.bufs, self.sems = idx_hbm, bufs, sems
        self.half = bufs[0].shape[0]
        self.step = pl.program_id(0)
        self.last = pl.num_programs(0) - 1

        @pl.when(self.step == 0)
        def _():
            for k in range(2):
                self._copy(self.step, k).start()

    def _copy(self, step, k):
        rows = pl.ds((2 * step + k) * self.half, self.half)
        return pltpu.make_async_copy(self.hbm.at[rows], self.bufs[k], self.sems.at[k])

    def gather(self, t, tab_ref, gbuf):
        k, local = divmod(t, self.half)
        if local == 0:
            self._copy(self.step, k).wait()
        _gather_rows(self.bufs[k], local, tab_ref, gbuf)
        if local == self.half - 1:
            self._copy(jnp.minimum(self.step + 1, self.last), k).start()

    def finish(self):
        @pl.when(self.step == self.last)
        def _():
            for k in range(2):
                self._copy(self.last, k).wait()


def _gathered_matrix(gbuf):
    planes = [pltpu.bitcast(gbuf[i * GATHER_PITCH:i * GATHER_PITCH + PEER_SLOTS, :], BF16) for i in range(WORD_ROWS)]
    return jnp.concatenate(planes, axis=1)


def _lhs16(rows, width):
    r = lax.broadcasted_iota(jnp.int32, (16, width), 0)
    out = jnp.zeros((16, width), F32)
    for k, v in enumerate(rows):
        out = jnp.where(r == k, v, out)
    return out.astype(BF16)


PEER_TOKENS_PER_STEP = 64
PIPELINE_LAG = 2


def _pipelined_tokens(tb, gather, compute, bufs):
    n = len(bufs)
    for t in range(tb + PIPELINE_LAG):
        if t < tb:
            gather(t, bufs[t % n])
        if t >= PIPELINE_LAG:
            compute(t - PIPELINE_LAG, bufs[(t - PIPELINE_LAG) % n])


def _peer_u_kernel(idx_hbm, h_ref, tab_ref, g_ref, act_ref, hhi_ref, hlo_ref, c_ref, ga0, ga1, gb0, gb1,
                   idx_a, idx_b, idx_sems, *, tb):
    h = h_ref[...]
    hi = h.astype(BF16).astype(F32)
    hhi_ref[...] = hi
    hlo_ref[...] = h - hi
    even = (lax.broadcasted_iota(jnp.int32, (SUBLANES, 2 * PEER_SLOTS), 1) % 2) == 0
    indices = _StagedIndices(idx_hbm, (idx_a, idx_b), idx_sems)

    def compute(t, gbuf):
        row = pl.ds(t, 1)
        lhs = _lhs16([hhi_ref[row, :HALF], hhi_ref[row, HALF:], hlo_ref[row, :HALF], hlo_ref[row, HALF:]], HALF)
        res = _dot_nt(lhs, _gathered_matrix(gbuf))[:SUBLANES]
        s = res + pltpu.roll(res, 6, 0)
        c_ref[row, :] = jnp.where(even, s, pltpu.roll(s, 7, 0))[0:1]

    _pipelined_tokens(tb, lambda t, g: indices.gather(t, tab_ref, g), compute, (ga0, ga1, gb0, gb1))

    c = c_ref[...]
    a = c + pltpu.roll(c, 2 * PEER_SLOTS - 1, 1)
    gelu = 0.5 * a * (1.0 + lax.erf(a * np.float32(np.sqrt(0.5))))
    lane_even = (lax.broadcasted_iota(jnp.int32, a.shape, 1) % 2) == 0
    act_ref[...] = jnp.where(lane_even, gelu * g_ref[...], 0.0)
    indices.finish()


def _gather_scratch(tb):
    return ([pltpu.VMEM((WORD_ROWS * GATHER_PITCH, LANES), jnp.uint32)] * 4
            + [pltpu.SMEM((tb // 2, PEER_SLOTS), jnp.int32)] * 2 + [pltpu.SemaphoreType.DMA((2,))])


def _peer_u(idx, h2, tab, gates, tb=PEER_TOKENS_PER_STEP):
    T, D = h2.shape
    wide = pl.BlockSpec((tb, 2 * PEER_SLOTS), lambda i: (i, 0))
    return pl.pallas_call(
        functools.partial(_peer_u_kernel, tb=tb),
        grid=(T // tb,),
        in_specs=[pl.BlockSpec(memory_space=pl.ANY),
                  pl.BlockSpec((tb, D), lambda i: (i, 0)), _const_spec(tab.shape), wide],
        out_specs=wide,
        out_shape=jax.ShapeDtypeStruct((T, 2 * PEER_SLOTS), F32),
        scratch_shapes=[pltpu.VMEM((tb, D), F32)] * 2 + [pltpu.VMEM((tb, 2 * PEER_SLOTS), F32)] + _gather_scratch(tb),
        compiler_params=_cparams(1),
        name="peer_u",
    )(idx, h2, tab, gates)


def _peer_v_kernel(idx_hbm, act_ref, x_ref, tab_ref, o_ref, ahi_ref, alo_ref, bhi_ref, blo_ref, acc_ref,
                   ga0, ga1, gb0, gb1, idx_a, idx_b, idx_sems, *, tb):
    a = act_ref[...]
    b = pltpu.roll(a, 1, 1)
    for src, hi_ref, lo_ref in ((a, ahi_ref, alo_ref), (b, bhi_ref, blo_ref)):
        hi = src.astype(BF16).astype(F32)
        hi_ref[...] = hi
        lo_ref[...] = src - hi
    indices = _StagedIndices(idx_hbm, (idx_a, idx_b), idx_sems)

    def compute(t, gbuf):
        row = pl.ds(t, 1)
        lhs = _lhs16([ahi_ref[row, :], bhi_ref[row, :], alo_ref[row, :], blo_ref[row, :]], 2 * PEER_SLOTS)
        res = _dot(lhs, _gathered_matrix(gbuf))[:SUBLANES]
        s = res + pltpu.roll(res, 6, 0)
        acc_ref[row, :HALF] = s[0:1]
        acc_ref[row, HALF:] = s[1:2]

    _pipelined_tokens(tb, lambda t, g: indices.gather(t, tab_ref, g), compute, (ga0, ga1, gb0, gb1))
    o_ref[...] = x_ref[...] + acc_ref[...]
    indices.finish()


def _peer_v(idx, act, x2d, tab, tb=PEER_TOKENS_PER_STEP):
    T, D = x2d.shape
    tok = pl.BlockSpec((tb, D), lambda i: (i, 0))
    wide = pl.BlockSpec((tb, 2 * PEER_SLOTS), lambda i: (i, 0))
    return pl.pallas_call(
        functools.partial(_peer_v_kernel, tb=tb),
        grid=(T // tb,),
        in_specs=[pl.BlockSpec(memory_space=pl.ANY), wide, tok, _const_spec(tab.shape)],
        out_specs=tok,
        out_shape=jax.ShapeDtypeStruct((T, D), F32),
        scratch_shapes=[pltpu.VMEM((tb, 2 * PEER_SLOTS), F32)] * 4 + [pltpu.VMEM((tb, D), F32)] + _gather_scratch(tb),
        compiler_params=_cparams(1),
        name="peer_v",
    )(idx, act, x2d, tab)


def _ple_kernel(x_ref, p_ref, g_ref, wg_ref, wp_ref, o_ref):
    x = x_ref[...]
    h = _rms(x, g_ref[...]).astype(BF16)
    o_ref[...] = x + jax.nn.sigmoid(_dot(h, wg_ref[...])) * _dot(p_ref[...].astype(BF16), wp_ref[...])


def _ple(x2d, p2d, gain, wg, wp, tm=256):
    T, D = x2d.shape
    row = lambda w: pl.BlockSpec((tm, w), lambda i: (i, 0))
    return pl.pallas_call(
        _ple_kernel,
        grid=(T // tm,),
        in_specs=[row(D), row(p2d.shape[1]), _const_spec((1, D)), _const_spec(wg.shape), _const_spec(wp.shape)],
        out_specs=row(D),
        out_shape=jax.ShapeDtypeStruct((T, D), F32),
        compiler_params=_cparams(1),
        name="ple",
    )(x2d, p2d, gain, wg, wp)


def _rotary_tables(seq):
    half = ROT_DIM // 2
    inv_freq = jnp.power(jnp.float32(ROPE_THETA), -jnp.arange(half, dtype=F32) * 2.0 / ROT_DIM)
    ang = jnp.arange(seq).astype(F32)[:, None] * inv_freq[None, :]
    cos, sin = jnp.cos(ang), jnp.sin(ang)
    pad = HEAD_DIM - ROT_DIM
    ra = jnp.concatenate([cos, cos, jnp.ones((seq, pad), F32)], axis=1)
    rb = jnp.concatenate([-sin, sin, jnp.zeros((seq, pad), F32)], axis=1)
    return jnp.tile(ra, (1, DIL_HEADS_PER_GROUP)), jnp.tile(rb, (1, DIL_HEADS_PER_GROUP))


def kernel(x, p, norm_mix, w_in, qk_norm_na, na_rel_bias, qk_norm_dil, w_branch_na, w_branch_dil, w_out, norm_ffn,
           peer_w_query, peer_sub_keys, peer_expert_u, peer_expert_v, norm_ple, w_ple_gate, w_ple):
    B, S, D = x.shape
    T = B * S
    depth = w_in.shape[0]
    wna = NA_HEADS * HEAD_DIM
    wdil = len(DIL_CONFIGS) * DIL_GROUP_WIDTH
    rot_a, rot_b = _rotary_tables(S)
    x2d = x.reshape(T, D)
    for i in range(depth):
        wi = w_in[i].astype(BF16)
        nna = jnp.tile(qk_norm_na[i], (1, NA_HEADS))
        ndil = jnp.tile(qk_norm_dil[i], (1, DIL_HEADS_PER_GROUP))
        (qa, ka, va, q1, k1, v1, q2, k2, v2, q3, k3, v3, sgn, sgd) = _in_proj(
            x2d, norm_mix[i][None], wi[:, :3 * wna], wi[:, 3 * wna:3 * wna + 3 * wdil], wi[:, 3 * wna + 3 * wdil:],
            nna, ndil, rot_a, rot_b, S)

        ona = _na_attn(qa, ka, va, _na_bias_table(na_rel_bias[i]), B, S)
        dil = [_dil_attn(q, k, v, B, S, window, dilation)
               for (q, k, v), (window, dilation) in zip(((q1, k1, v1), (q2, k2, v2), (q3, k3, v3)), DIL_CONFIGS)]

        x1, h2 = _merge(x2d, ona, [o for o, _ in dil], [l for _, l in dil], sgn, sgd,
                        w_branch_na[i].astype(BF16), w_branch_dil[i].astype(BF16), w_out[i].astype(BF16),
                        norm_ffn[i][None])

        e_t, g_t = _peer_topk(h2, peer_w_query[i].astype(BF16), peer_sub_keys[i, 0].astype(BF16),
                              peer_sub_keys[i, 1].astype(BF16))
        idx = e_t.T * WORD_ROWS
        gates = jnp.stack([g_t.T, jnp.zeros((T, PEER_SLOTS), F32)], axis=-1).reshape(T, 2 * PEER_SLOTS)
        act = _peer_u(idx, h2, _pack_table(peer_expert_u[i]), gates)
        x2 = _peer_v(idx, act, x1, _pack_table(peer_expert_v[i]))

        x2d = _ple(x2, p[i].reshape(T, -1), norm_ple[i][None], w_ple_gate[i].astype(BF16), w_ple[i].astype(BF16))
    return x2d.reshape(B, S, D)
```

```python
import functools

import numpy as np
import jax
import jax.numpy as jnp
from jax import lax
from jax.experimental import pallas as pl
from jax.experimental.pallas import tpu as pltpu

F32 = jnp.float32
BF16 = jnp.bfloat16

HEAD_DIM = 64
GRID_W = 64
NA_HEADS = 8
NA_WIN_ROWS = 8
NA_WIN_COLS = 16
DIL_CONFIGS = ((128, 1), (512, 4), (2048, 16))
DIL_HEADS_PER_GROUP = 4
DIL_GROUP_WIDTH = DIL_HEADS_PER_GROUP * HEAD_DIM
ROT_DIM = HEAD_DIM // 4
ROPE_THETA = 500000.0
PEER_HEADS = 8
PEER_NKEYS = 128
PEER_QDIM = 256
PEER_TOPK = 16
PEER_SLOTS = PEER_HEADS * PEER_TOPK
RMS_EPS = 1e-6
NEG_BIG = -1e30

LANES = 128
SUBLANES = 8
VMEM_LIMIT_BYTES = 48 * 1024 * 1024


def _cparams(n_axes):
    return pltpu.CompilerParams(dimension_semantics=("arbitrary",) * n_axes, vmem_limit_bytes=VMEM_LIMIT_BYTES)


def _const_spec(shape):
    nd = len(shape)
    return pl.BlockSpec(shape, lambda *_: (0,) * nd, pipeline_mode=pl.Buffered(1))


def _rms(x, gain):
    return x * lax.rsqrt(jnp.mean(x * x, axis=-1, keepdims=True) + RMS_EPS) * gain


def _dot(a, b):
    return jnp.dot(a, b, preferred_element_type=F32)


def _dot_nt(a, b):
    return lax.dot_general(a, b, (((1,), (1,)), ((), ())), preferred_element_type=F32)


def _relayout_scratch(tm):
    return pltpu.VMEM((DIL_GROUP_WIDTH // LANES, tm, LANES), F32)


def _to_residue_layout(x, scr_ref, d):
    tm = x.shape[0]
    slabs = DIL_GROUP_WIDTH // LANES
    for c in range(slabs):
        scr_ref[c] = x[:, c * LANES:(c + 1) * LANES]
    return jnp.concatenate([scr_ref[c, pl.ds(r, tm // d, stride=d), :] for r in range(d) for c in range(slabs)], axis=1)


def _from_residue_layout(y, scr_ref, d):
    rows = y.shape[0]
    slabs = DIL_GROUP_WIDTH // LANES
    for r in range(d):
        for c in range(slabs):
            lo = r * DIL_GROUP_WIDTH + c * LANES
            scr_ref[c, pl.ds(r, rows, stride=d), :] = y[:, lo:lo + LANES]
    return jnp.concatenate([scr_ref[c] for c in range(slabs)], axis=1)


def _head_rms(q, bd_ref, gain):
    outs = []
    for c in range(q.shape[1] // 256):
        qc = q[:, c * 256:(c + 1) * 256]
        sq = qc * qc
        hi = sq.astype(BF16)
        lo = (sq - hi.astype(F32)).astype(BF16)
        ssq = _dot(hi, bd_ref[...]) + _dot(lo, bd_ref[...])
        outs.append(qc * lax.rsqrt(ssq * (1.0 / HEAD_DIM) + RMS_EPS))
    return jnp.concatenate(outs, axis=1) * gain


def _rotary(q, ra, rb):
    lane = lax.broadcasted_iota(jnp.int32, q.shape, 1) % HEAD_DIM
    partner = jnp.where(lane < ROT_DIM // 2, pltpu.roll(q, 256 - ROT_DIM // 2, 1), pltpu.roll(q, ROT_DIM // 2, 1))
    return q * ra + partner * rb


def _in_proj_kernel(x_ref, g_ref, wna_ref, wdil_ref, wgate_ref, nna_ref, ndil_ref, ra_ref, rb_ref, bd_ref,
                    qa_ref, ka_ref, va_ref,
                    q1_ref, k1_ref, v1_ref, q2_ref, k2_ref, v2_ref, q3_ref, k3_ref, v3_ref,
                    sgn_ref, sgd_ref, *relayout_scratch):
    scratch = iter(relayout_scratch)
    h = _rms(x_ref[...], g_ref[...]).astype(BF16)
    scale = HEAD_DIM ** -0.5

    na = _dot(h, wna_ref[...])
    wna = NA_HEADS * HEAD_DIM
    qa_ref[...] = (_head_rms(na[:, :wna], bd_ref, nna_ref[0:1, :]) * scale).astype(BF16)
    ka_ref[...] = _head_rms(na[:, wna:2 * wna], bd_ref, nna_ref[1:2, :]).astype(BF16)
    va_ref[...] = na[:, 2 * wna:].astype(BF16)

    dil = _dot(h, wdil_ref[...])
    wd = len(DIL_CONFIGS) * DIL_GROUP_WIDTH
    ra = ra_ref[...]
    rb = rb_ref[...]
    q_refs = (q1_ref, q2_ref, q3_ref)
    k_refs = (k1_ref, k2_ref, k3_ref)
    v_refs = (v1_ref, v2_ref, v3_ref)
    for g in range(len(DIL_CONFIGS)):
        lo = g * DIL_GROUP_WIDTH
        hi = lo + DIL_GROUP_WIDTH
        q = _head_rms(dil[:, lo:hi], bd_ref, ndil_ref[0:1, :])
        k = _head_rms(dil[:, wd + lo:wd + hi], bd_ref, ndil_ref[1:2, :])
        dilation = DIL_CONFIGS[g][1]
        for ref, val in ((q_refs[g], _rotary(q, ra, rb) * scale), (k_refs[g], _rotary(k, ra, rb)),
                         (v_refs[g], dil[:, 2 * wd + lo:2 * wd + hi])):
            if dilation > 1:
                val = _to_residue_layout(val, next(scratch), dilation)
            ref[...] = val.astype(BF16)

    gate = _dot(h, wgate_ref[...])
    d = sgn_ref.shape[1]
    sgn_ref[...] = jax.nn.sigmoid(gate[:, :d]).astype(BF16)
    sgd_ref[...] = jax.nn.sigmoid(gate[:, d:]).astype(BF16)


def _in_proj(x2d, gain, w_na, w_dil, w_gate, nna, ndil, rot_a, rot_b, seq, tm=256):
    T, D = x2d.shape
    bd = jnp.asarray(np.kron(np.eye(256 // HEAD_DIM), np.ones((HEAD_DIM, HEAD_DIM))), BF16)
    wna = NA_HEADS * HEAD_DIM
    nseq = seq // tm
    row = lambda w: pl.BlockSpec((tm, w), lambda i: (i, 0))
    rot = pl.BlockSpec((tm, DIL_GROUP_WIDTH), lambda i: (i % nseq, 0))
    dil_shapes = [(T // d, d * DIL_GROUP_WIDTH) for _, d in DIL_CONFIGS for _ in range(3)]
    dil_specs = [pl.BlockSpec((tm // d, d * DIL_GROUP_WIDTH), lambda i: (i, 0)) for _, d in DIL_CONFIGS for _ in range(3)]
    outs = [jax.ShapeDtypeStruct((T, wna), BF16)] * 3 + [jax.ShapeDtypeStruct(s, BF16) for s in dil_shapes] \
        + [jax.ShapeDtypeStruct((T, D), BF16)] * 2
    n_relayout = 3 * sum(d > 1 for _, d in DIL_CONFIGS)
    return pl.pallas_call(
        _in_proj_kernel,
        grid=(T // tm,),
        in_specs=[row(D), _const_spec((1, D)), _const_spec(w_na.shape), _const_spec(w_dil.shape),
                  _const_spec(w_gate.shape), _const_spec(nna.shape), _const_spec(ndil.shape), rot, rot,
                  _const_spec(bd.shape)],
        out_specs=[row(wna)] * 3 + dil_specs + [row(D)] * 2,
        out_shape=outs,
        scratch_shapes=[_relayout_scratch(tm)] * n_relayout,
        compiler_params=_cparams(1),
        name="in_proj",
    )(x2d, gain, w_na, w_dil, w_gate, nna, ndil, rot_a, rot_b, bd)


def _na_row_start(r, rows):
    return np.clip(r - NA_WIN_ROWS // 2, 0, rows - NA_WIN_ROWS)


NA_GROUP_ROWS = 4
NA_GROUP_WINDOW = 12


def _na_group_start(g, rows, xp=jnp):
    return xp.clip(g * NA_GROUP_ROWS - NA_WIN_ROWS // 2, 0, rows - NA_GROUP_WINDOW)


def _head_pair_rows(x, first):
    zero = jnp.zeros_like(x)
    return jnp.concatenate([jnp.where(first, x, zero), jnp.where(first, zero, x)], axis=0)


def _na_kernel(q_ref, k_ref, v_ref, b_ref, o_ref, *, rows):
    nq = NA_GROUP_ROWS * GRID_W
    nk = NA_GROUP_WINDOW * GRID_W
    pair = 2 * HEAD_DIM
    first = lax.broadcasted_iota(jnp.int32, (nq, pair), 1) < HEAD_DIM
    off = pl.multiple_of(_na_group_start(pl.program_id(1), rows) * GRID_W, GRID_W)
    kw = k_ref[pl.ds(off, nk), :]
    vw = v_ref[pl.ds(off, nk), :]
    q = q_ref[...]
    outs = []
    for hp in range(NA_HEADS // 2):
        sl = slice(hp * pair, (hp + 1) * pair)
        s = _dot_nt(_head_pair_rows(q[:, sl], first), kw[:, sl]) + b_ref[0, hp]
        m = jnp.max(s, axis=-1, keepdims=True)
        p = jnp.exp(s - m)
        l = jnp.sum(p, axis=-1, keepdims=True)
        o = _dot(p.astype(BF16), vw[:, sl]) / l
        outs.append(jnp.where(first, o[:nq], o[nq:]))
    o_ref[...] = jnp.concatenate(outs, axis=1).astype(BF16)


def _na_bias_table(rpb, rows):
    qc = np.arange(GRID_W)[:, None]
    kc = np.arange(GRID_W)[None, :]
    cs = np.clip(qc - NA_WIN_COLS // 2, 0, GRID_W - NA_WIN_COLS)
    col_ok = (kc >= cs) & (kc < cs + NA_WIN_COLS)
    dc = np.clip(kc - qc + NA_WIN_COLS - 1, 0, 2 * NA_WIN_COLS - 2)
    colb = rpb.astype(F32)[:, :, dc]
    H = rpb.shape[0]
    ngroups = rows // NA_GROUP_ROWS
    assert ngroups >= 3 and rows >= NA_GROUP_WINDOW
    tabs = []
    for g in (0, 1, ngroups - 1):
        key_row = _na_group_start(g, rows, np) + np.arange(NA_GROUP_WINDOW)
        per_row = []
        for r in range(g * NA_GROUP_ROWS, (g + 1) * NA_GROUP_ROWS):
            rs = _na_row_start(r, rows)
            row_ok = (key_row >= rs) & (key_row < rs + NA_WIN_ROWS)
            dr = np.clip(key_row - r + NA_WIN_ROWS - 1, 0, 2 * NA_WIN_ROWS - 2)
            ok = row_ok[:, None, None] & col_ok[None]
            b = jnp.where(ok[None], colb[:, dr], NEG_BIG)
            per_row.append(b.transpose(0, 2, 1, 3).reshape(H, GRID_W, NA_GROUP_WINDOW * GRID_W))
        t = jnp.stack(per_row, axis=1)
        tabs.append(t.reshape(H // 2, 2 * NA_GROUP_ROWS * GRID_W, NA_GROUP_WINDOW * GRID_W))
    return jnp.stack(tabs)


def _na_attn(qa, ka, va, bias_tab, batch, seq):
    T, W = qa.shape
    rows = seq // GRID_W
    ngroups = rows // NA_GROUP_ROWS

    def bias_idx(b, g):
        return (jnp.where(g == 0, 0, jnp.where(g == ngroups - 1, 2, 1)), 0, 0, 0)

    kv = pl.BlockSpec((seq, W), lambda b, g: (b, 0))
    qo = pl.BlockSpec((NA_GROUP_ROWS * GRID_W, W), lambda b, g: (b * ngroups + g, 0))
    return pl.pallas_call(
        functools.partial(_na_kernel, rows=rows),
        grid=(batch, ngroups),
        in_specs=[qo, kv, kv, pl.BlockSpec((1,) + bias_tab.shape[1:], bias_idx)],
        out_specs=qo,
        out_shape=jax.ShapeDtypeStruct((T, W), BF16),
        compiler_params=_cparams(2),
        name="na_attn",
    )(qa, ka, va, bias_tab)


DIL_QUERY_BLOCK = 128
DIL_BLOCKS_PER_ITER = 8


def _dil_kernel(q_ref, k_ref, v_ref, o_ref, lse_ref, *, length, side, qb, tiles):
    kb = qb + 2 * side
    nblk = length // qb
    qi = lax.broadcasted_iota(jnp.int32, (2 * qb, kb), 0) % qb
    kj = lax.broadcasted_iota(jnp.int32, (2 * qb, kb), 1)
    rel = kj - qi
    first = lax.broadcasted_iota(jnp.int32, (qb, LANES), 1) < HEAD_DIM

    def block(tile, i):
        lanes = slice(tile * LANES, (tile + 1) * LANES)
        qs = pl.multiple_of(i * qb, qb)
        ws = pl.multiple_of(jnp.clip(qs - side, 0, length - kb), side)
        delta = rel + (ws - qs)
        band = (delta >= -side) & (delta <= side)
        q = q_ref[0, pl.ds(qs, qb), lanes]
        k = k_ref[0, pl.ds(ws, kb), lanes]
        v = v_ref[0, pl.ds(ws, kb), lanes]
        s = jnp.where(band, _dot_nt(_head_pair_rows(q, first), k), NEG_BIG)
        m = jnp.max(s, axis=-1, keepdims=True)
        p = jnp.exp(s - m)
        l = jnp.sum(p, axis=-1, keepdims=True)
        o = _dot(p.astype(BF16), v) / l
        lse = jnp.broadcast_to(m + jnp.log(l), (2 * qb, LANES))
        o_ref[0, pl.ds(qs, qb), lanes] = jnp.where(first, o[:qb], o[qb:])
        lse_ref[0, pl.ds(qs, qb), lanes] = jnp.where(first, lse[:qb], lse[qb:])

    per_iter = min(DIL_BLOCKS_PER_ITER, nblk)

    def several(it, _):
        for tile in range(tiles):
            for u in range(per_iter):
                block(tile, it * per_iter + u)
        return 0

    lax.fori_loop(0, nblk // per_iter, several, 0)


def _dil_attn(q, k, v, batch, seq, window, dilation):
    length = seq // dilation
    side = window // (2 * dilation)
    qb = min(DIL_QUERY_BLOCK, length - 2 * side)
    width = dilation * DIL_GROUP_WIDTH
    shp = (batch, length, width)
    tiles = max(1, DIL_BLOCKS_PER_ITER // (length // qb))
    spec = pl.BlockSpec((1, length, tiles * LANES), lambda b, j: (b, 0, j))
    o, lse = pl.pallas_call(
        functools.partial(_dil_kernel, length=length, side=side, qb=qb, tiles=tiles),
        grid=(batch, width // (tiles * LANES)),
        in_specs=[spec, spec, spec],
        out_specs=[spec, spec],
        out_shape=[jax.ShapeDtypeStruct(shp, F32)] * 2,
        compiler_params=_cparams(2),
        name=f"dil_attn_d{dilation}",
    )(q.reshape(shp), k.reshape(shp), v.reshape(shp))
    return o.reshape(q.shape), lse.reshape(q.shape)


def _merge_kernel(x_ref, ona_ref, o1_ref, o2_ref, o3_ref, l1_ref, l2_ref, l3_ref, sgn_ref, sgd_ref,
                  wna_ref, wdil_ref, wout_ref, g_ref, x1_ref, h2_ref, *relayout_scratch):
    scratch = iter(relayout_scratch)

    def token_order(ref, dilation):
        return ref[...] if dilation == 1 else _from_residue_layout(ref[...], next(scratch), dilation)

    o1, o2, o3 = (token_order(r, d) for r, (_, d) in zip((o1_ref, o2_ref, o3_ref), DIL_CONFIGS))
    l1, l2, l3 = (token_order(r, d) for r, (_, d) in zip((l1_ref, l2_ref, l3_ref), DIL_CONFIGS))
    m = jnp.maximum(jnp.maximum(l1, l2), l3)
    w1, w2, w3 = jnp.exp(l1 - m), jnp.exp(l2 - m), jnp.exp(l3 - m)
    od = (w1 * o1 + w2 * o2 + w3 * o3) / (w1 + w2 + w3)
    merged = (sgn_ref[...].astype(F32) * _dot(ona_ref[...], wna_ref[...])
              + sgd_ref[...].astype(F32) * _dot(od.astype(BF16), wdil_ref[...]))
    x1 = x_ref[...] + _dot(merged.astype(BF16), wout_ref[...])
    x1_ref[...] = x1
    h2_ref[...] = _rms(x1, g_ref[...])


def _merge(x2d, ona, os_, ls_, sgn, sgd, wna, wdil, wout, gain, tm=256):
    T, D = x2d.shape
    row = lambda w: pl.BlockSpec((tm, w), lambda i: (i, 0))
    residue = [pl.BlockSpec((tm // d, d * DIL_GROUP_WIDTH), lambda i: (i, 0)) for _, d in DIL_CONFIGS]
    n_relayout = 2 * sum(d > 1 for _, d in DIL_CONFIGS)
    return pl.pallas_call(
        _merge_kernel,
        grid=(T // tm,),
        in_specs=[row(D), row(ona.shape[1])] + residue * 2 + [row(D), row(D),
                  _const_spec(wna.shape), _const_spec(wdil.shape), _const_spec(wout.shape), _const_spec((1, D))],
        out_specs=[row(D), row(D)],
        out_shape=[jax.ShapeDtypeStruct((T, D), F32)] * 2,
        scratch_shapes=[_relayout_scratch(tm)] * n_relayout,
        compiler_params=_cparams(1),
        name="merge",
    )(x2d, ona, *os_, *ls_, sgn, sgd, wna, wdil, wout, gain)


def _topk_rows(s, k, payload=None):
    n = s.shape[0]
    row = lax.broadcasted_iota(jnp.int32, s.shape, 0)
    vals, idxs = [], []
    for _ in range(k):
        m = jnp.max(s, axis=0, keepdims=True)
        idx = jnp.min(jnp.where(s == m, row, n), axis=0, keepdims=True)
        sel = row == idx
        vals.append(m)
        idxs.append(idx if payload is None else jnp.max(jnp.where(sel, payload, -1), axis=0, keepdims=True))
        s = jnp.where(sel, -jnp.inf, s)
    return jnp.concatenate(vals, axis=0), jnp.concatenate(idxs, axis=0)


def _peer_topk_kernel(h_ref, wq_ref, k1_ref, k2_ref, e_ref, g_ref):
    q = _dot(h_ref[...].astype(BF16), wq_ref[...])
    half = PEER_QDIM // 2
    s1 = _dot_nt(k1_ref[...], q[:, :half].astype(BF16))
    s2 = _dot_nt(k2_ref[...], q[:, half:].astype(BF16))
    v1, i1 = _topk_rows(s1, PEER_TOPK)
    v2, i2 = _topk_rows(s2, PEER_TOPK)
    keep = [PEER_TOPK // (i + 1) for i in range(PEER_TOPK)]
    pad = -sum(keep) % SUBLANES
    tm = v1.shape[1]
    cand = jnp.concatenate([v1[i:i + 1] + v2[:n] for i, n in enumerate(keep)]
                           + [jnp.full((pad, tm), -jnp.inf, F32)], axis=0)
    cidx = jnp.concatenate([i1[i:i + 1] * PEER_NKEYS + i2[:n] for i, n in enumerate(keep)]
                           + [jnp.full((pad, tm), -1, jnp.int32)], axis=0)
    sc, e = _topk_rows(cand, PEER_TOPK, payload=cidx)
    p = jnp.exp(sc - sc[0:1])
    e_ref[...] = e
    g_ref[...] = p / jnp.sum(p, axis=0, keepdims=True)


def _peer_topk(h2, wq, k1, k2, tm=512):
    T, D = h2.shape
    out = pl.BlockSpec((PEER_TOPK, tm), lambda i, h: (h, i))
    return pl.pallas_call(
        _peer_topk_kernel,
        grid=(T // tm, PEER_HEADS),
        in_specs=[pl.BlockSpec((tm, D), lambda i, h: (i, 0)), pl.BlockSpec((D, PEER_QDIM), lambda i, h: (0, h)),
                  _const_spec(k1.shape), _const_spec(k2.shape)],
        out_specs=[out, out],
        out_shape=[jax.ShapeDtypeStruct((PEER_SLOTS, T), jnp.int32), jax.ShapeDtypeStruct((PEER_SLOTS, T), F32)],
        compiler_params=_cparams(2),
        name="peer_topk",
    )(h2, wq, k1, k2)


WORD_ROWS = 4
GATHER_PITCH = 136
HALF = 512


def _pack_table(w):
    e, d = w.shape
    wb = w.astype(BF16)
    pairs = jnp.stack([wb[:, :HALF], wb[:, HALF:]], axis=-1)
    return lax.bitcast_convert_type(pairs, jnp.uint32).reshape(e * WORD_ROWS, LANES)


INDEX_LOOKAHEAD = 8


def _gather_rows(idx_ref, t, tab_ref, gbuf):
    starts = [idx_ref[t, j] for j in range(INDEX_LOOKAHEAD)]
    for j in range(PEER_SLOTS):
        if j + INDEX_LOOKAHEAD < PEER_SLOTS:
            starts.append(idx_ref[t, j + INDEX_LOOKAHEAD])
        row0 = pl.multiple_of(starts[j], WORD_ROWS)
        gbuf[pl.ds(j, WORD_ROWS, stride=GATHER_PITCH), :] = tab_ref[pl.ds(row0, WORD_ROWS), :]


class _StagedIndices:
    def __init__(self, idx_hbm, bufs, sems):
        self.hbm, self.bufs, self.sems = idx_hbm, bufs, sems
        self.half = bufs[0].shape[0]
        self.step = pl.program_id(0)
        self.last = pl.num_programs(0) - 1

        @pl.when(self.step == 0)
        def _():
            for k in range(2):
                self._copy(self.step, k).start()

    def _copy(self, step, k):
        rows = pl.ds((2 * step + k) * self.half, self.half)
        return pltpu.make_async_copy(self.hbm.at[rows], self.bufs[k], self.sems.at[k])

    def gather(self, t, tab_ref, gbuf):
        k, local = divmod(t, self.half)
        if local == 0:
            self._copy(self.step, k).wait()
        _gather_rows(self.bufs[k], local, tab_ref, gbuf)
        if local == self.half - 1:
            self._copy(jnp.minimum(self.step + 1, self.last), k).start()

    def finish(self):
        @pl.when(self.step == self.last)
        def _():
            for k in range(2):
                self._copy(self.last, k).wait()


def _gathered_matrix(gbuf):
    planes = [pltpu.bitcast(gbuf[i * GATHER_PITCH:i * GATHER_PITCH + PEER_SLOTS, :], BF16) for i in range(WORD_ROWS)]
    return jnp.concatenate(planes, axis=1)


def _lhs16(rows, width):
    r = lax.broadcasted_iota(jnp.int32, (16, width), 0)
    out = jnp.zeros((16, width), F32)
    for k, v in enumerate(rows):
        out = jnp.where(r == k, v, out)
    return out.astype(BF16)


PEER_TOKENS_PER_STEP = 64
PIPELINE_LAG = 2


def _pipelined_tokens(tb, gather, compute, bufs):
    n = len(bufs)
    for t in range(tb + PIPELINE_LAG):
        if t < tb:
            gather(t, bufs[t % n])
        if t >= PIPELINE_LAG:
            compute(t - PIPELINE_LAG, bufs[(t - PIPELINE_LAG) % n])


def _peer_u_kernel(idx_hbm, h_ref, tab_ref, g_ref, act_ref, hhi_ref, hlo_ref, c_ref, ga0, ga1, gb0, gb1,
                   idx_a, idx_b, idx_sems, *, tb):
    h = h_ref[...]
    hi = h.astype(BF16).astype(F32)
    hhi_ref[...] = hi
    hlo_ref[...] = h - hi
    even = (lax.broadcasted_iota(jnp.int32, (SUBLANES, 2 * PEER_SLOTS), 1) % 2) == 0
    indices = _StagedIndices(idx_hbm, (idx_a, idx_b), idx_sems)

    def compute(t, gbuf):
        row = pl.ds(t, 1)
        lhs = _lhs16([hhi_ref[row, :HALF], hhi_ref[row, HALF:], hlo_ref[row, :HALF], hlo_ref[row, HALF:]], HALF)
        res = _dot_nt(lhs, _gathered_matrix(gbuf))[:SUBLANES]
        s = res + pltpu.roll(res, 6, 0)
        c_ref[row, :] = jnp.where(even, s, pltpu.roll(s, 7, 0))[0:1]

    _pipelined_tokens(tb, lambda t, g: indices.gather(t, tab_ref, g), compute, (ga0, ga1, gb0, gb1))

    c = c_ref[...]
    a = c + pltpu.roll(c, 2 * PEER_SLOTS - 1, 1)
    gelu = 0.5 * a * (1.0 + lax.erf(a * np.float32(np.sqrt(0.5))))
    lane_even = (lax.broadcasted_iota(jnp.int32, a.shape, 1) % 2) == 0
    act_ref[...] = jnp.where(lane_even, gelu * g_ref[...], 0.0)
    indices.finish()


def _gather_scratch(tb):
    return ([pltpu.VMEM((WORD_ROWS * GATHER_PITCH, LANES), jnp.uint32)] * 4
            + [pltpu.SMEM((tb // 2, PEER_SLOTS), jnp.int32)] * 2 + [pltpu.SemaphoreType.DMA((2,))])


def _peer_u(idx, h2, tab, gates, tb=PEER_TOKENS_PER_STEP):
    T, D = h2.shape
    wide = pl.BlockSpec((tb, 2 * PEER_SLOTS), lambda i: (i, 0))
    return pl.pallas_call(
        functools.partial(_peer_u_kernel, tb=tb),
        grid=(T // tb,),
        in_specs=[pl.BlockSpec(memory_space=pl.ANY),
                  pl.BlockSpec((tb, D), lambda i: (i, 0)), _const_spec(tab.shape), wide],
        out_specs=wide,
        out_shape=jax.ShapeDtypeStruct((T, 2 * PEER_SLOTS), F32),
        scratch_shapes=[pltpu.VMEM((tb, D), F32)] * 2 + [pltpu.VMEM((tb, 2 * PEER_SLOTS), F32)] + _gather_scratch(tb),
        compiler_params=_cparams(1),
        name="peer_u",
    )(idx, h2, tab, gates)


def _peer_v_kernel(idx_hbm, act_ref, x_ref, tab_ref, o_ref, ahi_ref, alo_ref, bhi_ref, blo_ref, acc_ref,
                   ga0, ga1, gb0, gb1, idx_a, idx_b, idx_sems, *, tb):
    a = act_ref[...]
    b = pltpu.roll(a, 1, 1)
    for src, hi_ref, lo_ref in ((a, ahi_ref, alo_ref), (b, bhi_ref, blo_ref)):
        hi = src.astype(BF16).astype(F32)
        hi_ref[...] = hi
        lo_ref[...] = src - hi
    indices = _StagedIndices(idx_hbm, (idx_a, idx_b), idx_sems)

    def compute(t, gbuf):
        row = pl.ds(t, 1)
        lhs = _lhs16([ahi_ref[row, :], bhi_ref[row, :], alo_ref[row, :], blo_ref[row, :]], 2 * PEER_SLOTS)
        res = _dot(lhs, _gathered_matrix(gbuf))[:SUBLANES]
        s = res + pltpu.roll(res, 6, 0)
        acc_ref[row, :HALF] = s[0:1]
        acc_ref[row, HALF:] = s[1:2]

    _pipelined_tokens(tb, lambda t, g: indices.gather(t, tab_ref, g), compute, (ga0, ga1, gb0, gb1))
    o_ref[...] = x_ref[...] + acc_ref[...]
    indices.finish()


def _peer_v(idx, act, x2d, tab, tb=PEER_TOKENS_PER_STEP):
    T, D = x2d.shape
    tok = pl.BlockSpec((tb, D), lambda i: (i, 0))
    wide = pl.BlockSpec((tb, 2 * PEER_SLOTS), lambda i: (i, 0))
    return pl.pallas_call(
        functools.partial(_peer_v_kernel, tb=tb),
        grid=(T // tb,),
        in_specs=[pl.BlockSpec(memory_space=pl.ANY), wide, tok, _const_spec(tab.shape)],
        out_specs=tok,
        out_shape=jax.ShapeDtypeStruct((T, D), F32),
        scratch_shapes=[pltpu.VMEM((tb, 2 * PEER_SLOTS), F32)] * 4 + [pltpu.VMEM((tb, D), F32)] + _gather_scratch(tb),
        compiler_params=_cparams(1),
        name="peer_v",
    )(idx, act, x2d, tab)


def _ple_kernel(x_ref, p_ref, g_ref, wg_ref, wp_ref, o_ref):
    x = x_ref[...]
    h = _rms(x, g_ref[...]).astype(BF16)
    o_ref[...] = x + jax.nn.sigmoid(_dot(h, wg_ref[...])) * _dot(p_ref[...].astype(BF16), wp_ref[...])


def _ple(x2d, p2d, gain, wg, wp, tm=256):
    T, D = x2d.shape
    row = lambda w: pl.BlockSpec((tm, w), lambda i: (i, 0))
    return pl.pallas_call(
        _ple_kernel,
        grid=(T // tm,),
        in_specs=[row(D), row(p2d.shape[1]), _const_spec((1, D)), _const_spec(wg.shape), _const_spec(wp.shape)],
        out_specs=row(D),
        out_shape=jax.ShapeDtypeStruct((T, D), F32),
        compiler_params=_cparams(1),
        name="ple",
    )(x2d, p2d, gain, wg, wp)


def _rotary_tables(seq):
    half = ROT_DIM // 2
    inv_freq = jnp.power(jnp.float32(ROPE_THETA), -jnp.arange(half, dtype=F32) * 2.0 / ROT_DIM)
    ang = jnp.arange(seq).astype(F32)[:, None] * inv_freq[None, :]
    cos, sin = jnp.cos(ang), jnp.sin(ang)
    pad = HEAD_DIM - ROT_DIM
    ra = jnp.concatenate([cos, cos, jnp.ones((seq, pad), F32)], axis=1)
    rb = jnp.concatenate([-sin, sin, jnp.zeros((seq, pad), F32)], axis=1)
    return jnp.tile(ra, (1, DIL_HEADS_PER_GROUP)), jnp.tile(rb, (1, DIL_HEADS_PER_GROUP))


def kernel(x, p, norm_mix, w_in, qk_norm_na, na_rel_bias, qk_norm_dil, w_branch_na, w_branch_dil, w_out, norm_ffn,
           peer_w_query, peer_sub_keys, peer_expert_u, peer_expert_v, norm_ple, w_ple_gate, w_ple):
    B, S, D = x.shape
    T = B * S
    depth = w_in.shape[0]
    wna = NA_HEADS * HEAD_DIM
    wdil = len(DIL_CONFIGS) * DIL_GROUP_WIDTH
    rot_a, rot_b = _rotary_tables(S)
    x2d = x.reshape(T, D)
    for i in range(depth):
        wi = w_in[i].astype(BF16)
        nna = jnp.tile(qk_norm_na[i], (1, NA_HEADS))
        ndil = jnp.tile(qk_norm_dil[i], (1, DIL_HEADS_PER_GROUP))
        (qa, ka, va, q1, k1, v1, q2, k2, v2, q3, k3, v3, sgn, sgd) = _in_proj(
            x2d, norm_mix[i][None], wi[:, :3 * wna], wi[:, 3 * wna:3 * wna + 3 * wdil], wi[:, 3 * wna + 3 * wdil:],
            nna, ndil, rot_a, rot_b, S)

        ona = _na_attn(qa, ka, va, _na_bias_table(na_rel_bias[i], S // GRID_W), B, S)
        dil = [_dil_attn(q, k, v, B, S, window, dilation)
               for (q, k, v), (window, dilation) in zip(((q1, k1, v1), (q2, k2, v2), (q3, k3, v3)), DIL_CONFIGS)]

        x1, h2 = _merge(x2d, ona, [o for o, _ in dil], [l for _, l in dil], sgn, sgd,
                        w_branch_na[i].astype(BF16), w_branch_dil[i].astype(BF16), w_out[i].astype(BF16),
                        norm_ffn[i][None])

        e_t, g_t = _peer_topk(h2, peer_w_query[i].astype(BF16), peer_sub_keys[i, 0].astype(BF16),
                              peer_sub_keys[i, 1].astype(BF16))
        idx = e_t.T * WORD_ROWS
        gates = jnp.stack([g_t.T, jnp.zeros((T, PEER_SLOTS), F32)], axis=-1).reshape(T, 2 * PEER_SLOTS)
        act = _peer_u(idx, h2, _pack_table(peer_expert_u[i]), gates)
        x2 = _peer_v(idx, act, x1, _pack_table(peer_expert_v[i]))

        x2d = _ple(x2, p[i].reshape(T, -1), norm_ple[i][None], w_ple_gate[i].astype(BF16), w_ple[i].astype(BF16))
    return x2d.reshape(B, S, D)
```

```python
import functools

import numpy as np
import jax
import jax.numpy as jnp
from jax import lax
from jax.experimental import pallas as pl
from jax.experimental.pallas import tpu as pltpu

F32 = jnp.float32
BF16 = jnp.bfloat16

HEAD_DIM = 64
GRID_W = 64
NA_HEADS = 8
NA_WIN_ROWS = 8
NA_WIN_COLS = 16
DIL_CONFIGS = ((128, 1), (512, 4), (2048, 16))
DIL_HEADS_PER_GROUP = 4
DIL_GROUP_WIDTH = DIL_HEADS_PER_GROUP * HEAD_DIM
ROT_DIM = HEAD_DIM // 4
ROPE_THETA = 500000.0
PEER_HEADS = 8
PEER_NKEYS = 128
PEER_QDIM = 256
PEER_TOPK = 16
PEER_SLOTS = PEER_HEADS * PEER_TOPK
RMS_EPS = 1e-6
NEG_BIG = -1e30

LANES = 128
SUBLANES = 8
VMEM_LIMIT_BYTES = 48 * 1024 * 1024


def _cparams(n_axes):
    return pltpu.CompilerParams(dimension_semantics=("arbitrary",) * n_axes, vmem_limit_bytes=VMEM_LIMIT_BYTES)


def _const_spec(shape):
    nd = len(shape)
    return pl.BlockSpec(shape, lambda *_: (0,) * nd, pipeline_mode=pl.Buffered(1))


def _rms(x, gain):
    return x * lax.rsqrt(jnp.mean(x * x, axis=-1, keepdims=True) + RMS_EPS) * gain


def _dot(a, b):
    return jnp.dot(a, b, preferred_element_type=F32)


def _dot_nt(a, b):
    return lax.dot_general(a, b, (((1,), (1,)), ((), ())), preferred_element_type=F32)


def _relayout_scratch(tm):
    return pltpu.VMEM((DIL_GROUP_WIDTH // LANES, tm, LANES), F32)


def _to_residue_layout(x, scr_ref, d):
    tm = x.shape[0]
    slabs = DIL_GROUP_WIDTH // LANES
    for c in range(slabs):
        scr_ref[c] = x[:, c * LANES:(c + 1) * LANES]
    return jnp.concatenate([scr_ref[c, pl.ds(r, tm // d, stride=d), :] for r in range(d) for c in range(slabs)], axis=1)


def _from_residue_layout(y, scr_ref, d):
    rows = y.shape[0]
    slabs = DIL_GROUP_WIDTH // LANES
    for r in range(d):
        for c in range(slabs):
            lo = r * DIL_GROUP_WIDTH + c * LANES
            scr_ref[c, pl.ds(r, rows, stride=d), :] = y[:, lo:lo + LANES]
    return jnp.concatenate([scr_ref[c] for c in range(slabs)], axis=1)


def _head_rms(q, bd_ref, gain):
    outs = []
    for c in range(q.shape[1] // 256):
        qc = q[:, c * 256:(c + 1) * 256]
        sq = qc * qc
        hi = sq.astype(BF16)
        lo = (sq - hi.astype(F32)).astype(BF16)
        ssq = _dot(hi, bd_ref[...]) + _dot(lo, bd_ref[...])
        outs.append(qc * lax.rsqrt(ssq * (1.0 / HEAD_DIM) + RMS_EPS))
    return jnp.concatenate(outs, axis=1) * gain


def _rotary(q, ra, rb):
    lane = lax.broadcasted_iota(jnp.int32, q.shape, 1) % HEAD_DIM
    partner = jnp.where(lane < ROT_DIM // 2, pltpu.roll(q, 256 - ROT_DIM // 2, 1), pltpu.roll(q, ROT_DIM // 2, 1))
    return q * ra + partner * rb


def _in_proj_kernel(x_ref, g_ref, wna_ref, wdil_ref, wgate_ref, nna_ref, ndil_ref, ra_ref, rb_ref, bd_ref,
                    qa_ref, ka_ref, va_ref,
                    q1_ref, k1_ref, v1_ref, q2_ref, k2_ref, v2_ref, q3_ref, k3_ref, v3_ref,
                    sgn_ref, sgd_ref, *relayout_scratch):
    scratch = iter(relayout_scratch)
    h = _rms(x_ref[...], g_ref[...]).astype(BF16)
    scale = HEAD_DIM ** -0.5

    na = _dot(h, wna_ref[...])
    wna = NA_HEADS * HEAD_DIM
    qa_ref[...] = (_head_rms(na[:, :wna], bd_ref, nna_ref[0:1, :]) * scale).astype(BF16)
    ka_ref[...] = _head_rms(na[:, wna:2 * wna], bd_ref, nna_ref[1:2, :]).astype(BF16)
    va_ref[...] = na[:, 2 * wna:].astype(BF16)

    dil = _dot(h, wdil_ref[...])
    wd = len(DIL_CONFIGS) * DIL_GROUP_WIDTH
    ra = ra_ref[...]
    rb = rb_ref[...]
    q_refs = (q1_ref, q2_ref, q3_ref)
    k_refs = (k1_ref, k2_ref, k3_ref)
    v_refs = (v1_ref, v2_ref, v3_ref)
    for g in range(len(DIL_CONFIGS)):
        lo = g * DIL_GROUP_WIDTH
        hi = lo + DIL_GROUP_WIDTH
        q = _head_rms(dil[:, lo:hi], bd_ref, ndil_ref[0:1, :])
        k = _head_rms(dil[:, wd + lo:wd + hi], bd_ref, ndil_ref[1:2, :])
        dilation = DIL_CONFIGS[g][1]
        for ref, val in ((q_refs[g], _rotary(q, ra, rb) * scale), (k_refs[g], _rotary(k, ra, rb)),
                         (v_refs[g], dil[:, 2 * wd + lo:2 * wd + hi])):
            if dilation > 1:
                val = _to_residue_layout(val, next(scratch), dilation)
            ref[...] = val.astype(BF16)

    gate = _dot(h, wgate_ref[...])
    d = sgn_ref.shape[1]
    sgn_ref[...] = jax.nn.sigmoid(gate[:, :d]).astype(BF16)
    sgd_ref[...] = jax.nn.sigmoid(gate[:, d:]).astype(BF16)


def _in_proj(x2d, gain, w_na, w_dil, w_gate, nna, ndil, rot_a, rot_b, seq, tm=512):
    T, D = x2d.shape
    bd = jnp.asarray(np.kron(np.eye(256 // HEAD_DIM), np.ones((HEAD_DIM, HEAD_DIM))), BF16)
    wna = NA_HEADS * HEAD_DIM
    nseq = seq // tm
    row = lambda w: pl.BlockSpec((tm, w), lambda i: (i, 0))
    rot = pl.BlockSpec((tm, DIL_GROUP_WIDTH), lambda i: (i % nseq, 0))
    dil_shapes = [(T // d, d * DIL_GROUP_WIDTH) for _, d in DIL_CONFIGS for _ in range(3)]
    dil_specs = [pl.BlockSpec((tm // d, d * DIL_GROUP_WIDTH), lambda i: (i, 0)) for _, d in DIL_CONFIGS for _ in range(3)]
    outs = [jax.ShapeDtypeStruct((T, wna), BF16)] * 3 + [jax.ShapeDtypeStruct(s, BF16) for s in dil_shapes] \
        + [jax.ShapeDtypeStruct((T, D), BF16)] * 2
    n_relayout = 3 * sum(d > 1 for _, d in DIL_CONFIGS)
    return pl.pallas_call(
        _in_proj_kernel,
        grid=(T // tm,),
        in_specs=[row(D), _const_spec((1, D)), _const_spec(w_na.shape), _const_spec(w_dil.shape),
                  _const_spec(w_gate.shape), _const_spec(nna.shape), _const_spec(ndil.shape), rot, rot,
                  _const_spec(bd.shape)],
        out_specs=[row(wna)] * 3 + dil_specs + [row(D)] * 2,
        out_shape=outs,
        scratch_shapes=[_relayout_scratch(tm)] * n_relayout,
        compiler_params=_cparams(1),
        name="in_proj",
    )(x2d, gain, w_na, w_dil, w_gate, nna, ndil, rot_a, rot_b, bd)


def _na_row_start(r, rows):
    return np.clip(r - NA_WIN_ROWS // 2, 0, rows - NA_WIN_ROWS)


NA_GROUP_ROWS = 4
NA_GROUP_WINDOW = 12


def _na_group_start(g, rows, xp=jnp):
    return xp.clip(g * NA_GROUP_ROWS - NA_WIN_ROWS // 2, 0, rows - NA_GROUP_WINDOW)


def _head_pair_rows(x, first):
    zero = jnp.zeros_like(x)
    return jnp.concatenate([jnp.where(first, x, zero), jnp.where(first, zero, x)], axis=0)


def _na_kernel(q_ref, k_ref, v_ref, b_ref, o_ref, *, rows):
    nq = NA_GROUP_ROWS * GRID_W
    nk = NA_GROUP_WINDOW * GRID_W
    pair = 2 * HEAD_DIM
    first = lax.broadcasted_iota(jnp.int32, (nq, pair), 1) < HEAD_DIM
    off = pl.multiple_of(_na_group_start(pl.program_id(1), rows) * GRID_W, GRID_W)
    kw = k_ref[pl.ds(off, nk), :]
    vw = v_ref[pl.ds(off, nk), :]
    q = q_ref[...]
    outs = []
    for hp in range(NA_HEADS // 2):
        sl = slice(hp * pair, (hp + 1) * pair)
        s = _dot_nt(_head_pair_rows(q[:, sl], first), kw[:, sl]) + b_ref[0, hp]
        m = jnp.max(s, axis=-1, keepdims=True)
        p = jnp.exp(s - m)
        l = jnp.sum(p, axis=-1, keepdims=True)
        o = _dot(p.astype(BF16), vw[:, sl]) / l
        outs.append(jnp.where(first, o[:nq], o[nq:]))
    o_ref[...] = jnp.concatenate(outs, axis=1).astype(BF16)


def _na_bias_table(rpb, rows):
    qc = np.arange(GRID_W)[:, None]
    kc = np.arange(GRID_W)[None, :]
    cs = np.clip(qc - NA_WIN_COLS // 2, 0, GRID_W - NA_WIN_COLS)
    col_ok = (kc >= cs) & (kc < cs + NA_WIN_COLS)
    dc = np.clip(kc - qc + NA_WIN_COLS - 1, 0, 2 * NA_WIN_COLS - 2)
    colb = rpb.astype(F32)[:, :, dc]
    H = rpb.shape[0]
    ngroups = rows // NA_GROUP_ROWS
    assert ngroups >= 3 and rows >= NA_GROUP_WINDOW
    tabs = []
    for g in (0, 1, ngroups - 1):
        key_row = _na_group_start(g, rows, np) + np.arange(NA_GROUP_WINDOW)
        per_row = []
        for r in range(g * NA_GROUP_ROWS, (g + 1) * NA_GROUP_ROWS):
            rs = _na_row_start(r, rows)
            row_ok = (key_row >= rs) & (key_row < rs + NA_WIN_ROWS)
            dr = np.clip(key_row - r + NA_WIN_ROWS - 1, 0, 2 * NA_WIN_ROWS - 2)
            ok = row_ok[:, None, None] & col_ok[None]
            b = jnp.where(ok[None], colb[:, dr], NEG_BIG)
            per_row.append(b.transpose(0, 2, 1, 3).reshape(H, GRID_W, NA_GROUP_WINDOW * GRID_W))
        t = jnp.stack(per_row, axis=1)
        tabs.append(t.reshape(H // 2, 2 * NA_GROUP_ROWS * GRID_W, NA_GROUP_WINDOW * GRID_W))
    return jnp.stack(tabs)


def _na_attn(qa, ka, va, bias_tab, batch, seq):
    T, W = qa.shape
    rows = seq // GRID_W
    ngroups = rows // NA_GROUP_ROWS

    def bias_idx(b, g):
        return (jnp.where(g == 0, 0, jnp.where(g == ngroups - 1, 2, 1)), 0, 0, 0)

    kv = pl.BlockSpec((seq, W), lambda b, g: (b, 0))
    qo = pl.BlockSpec((NA_GROUP_ROWS * GRID_W, W), lambda b, g: (b * ngroups + g, 0))
    return pl.pallas_call(
        functools.partial(_na_kernel, rows=rows),
        grid=(batch, ngroups),
        in_specs=[qo, kv, kv, pl.BlockSpec((1,) + bias_tab.shape[1:], bias_idx)],
        out_specs=qo,
        out_shape=jax.ShapeDtypeStruct((T, W), BF16),
        compiler_params=_cparams(2),
        name="na_attn",
    )(qa, ka, va, bias_tab)


DIL_QUERY_BLOCK = 128
DIL_BLOCKS_PER_ITER = 8


def _dil_kernel(q_ref, k_ref, v_ref, o_ref, lse_ref, *, length, side, qb, tiles):
    kb = qb + 2 * side
    nblk = length // qb
    qi = lax.broadcasted_iota(jnp.int32, (2 * qb, kb), 0) % qb
    kj = lax.broadcasted_iota(jnp.int32, (2 * qb, kb), 1)
    rel = kj - qi
    first = lax.broadcasted_iota(jnp.int32, (qb, LANES), 1) < HEAD_DIM

    def block(tile, i):
        lanes = slice(tile * LANES, (tile + 1) * LANES)
        qs = pl.multiple_of(i * qb, qb)
        ws = pl.multiple_of(jnp.clip(qs - side, 0, length - kb), side)
        delta = rel + (ws - qs)
        band = (delta >= -side) & (delta <= side)
        q = q_ref[0, pl.ds(qs, qb), lanes]
        k = k_ref[0, pl.ds(ws, kb), lanes]
        v = v_ref[0, pl.ds(ws, kb), lanes]
        s = jnp.where(band, _dot_nt(_head_pair_rows(q, first), k), NEG_BIG)
        m = jnp.max(s, axis=-1, keepdims=True)
        p = jnp.exp(s - m)
        l = jnp.sum(p, axis=-1, keepdims=True)
        o = _dot(p.astype(BF16), v) / l
        lse = jnp.broadcast_to(m + jnp.log(l), (2 * qb, LANES))
        o_ref[0, pl.ds(qs, qb), lanes] = jnp.where(first, o[:qb], o[qb:])
        lse_ref[0, pl.ds(qs, qb), lanes] = jnp.where(first, lse[:qb], lse[qb:])

    per_iter = min(DIL_BLOCKS_PER_ITER, nblk)

    def several(it, _):
        for tile in range(tiles):
            for u in range(per_iter):
                block(tile, it * per_iter + u)
        return 0

    lax.fori_loop(0, nblk // per_iter, several, 0)


def _dil_attn(q, k, v, batch, seq, window, dilation):
    length = seq // dilation
    side = window // (2 * dilation)
    qb = min(DIL_QUERY_BLOCK, length - 2 * side)
    width = dilation * DIL_GROUP_WIDTH
    shp = (batch, length, width)
    tiles = max(1, DIL_BLOCKS_PER_ITER // (length // qb))
    spec = pl.BlockSpec((1, length, tiles * LANES), lambda b, j: (b, 0, j))
    o, lse = pl.pallas_call(
        functools.partial(_dil_kernel, length=length, side=side, qb=qb, tiles=tiles),
        grid=(batch, width // (tiles * LANES)),
        in_specs=[spec, spec, spec],
        out_specs=[spec, spec],
        out_shape=[jax.ShapeDtypeStruct(shp, F32)] * 2,
        compiler_params=_cparams(2),
        name=f"dil_attn_d{dilation}",
    )(q.reshape(shp), k.reshape(shp), v.reshape(shp))
    return o.reshape(q.shape), lse.reshape(q.shape)


def _merge_kernel(x_ref, ona_ref, o1_ref, o2_ref, o3_ref, l1_ref, l2_ref, l3_ref, sgn_ref, sgd_ref,
                  wna_ref, wdil_ref, wout_ref, g_ref, x1_ref, h2_ref, *relayout_scratch):
    scratch = iter(relayout_scratch)

    def token_order(ref, dilation):
        return ref[...] if dilation == 1 else _from_residue_layout(ref[...], next(scratch), dilation)

    o1, o2, o3 = (token_order(r, d) for r, (_, d) in zip((o1_ref, o2_ref, o3_ref), DIL_CONFIGS))
    l1, l2, l3 = (token_order(r, d) for r, (_, d) in zip((l1_ref, l2_ref, l3_ref), DIL_CONFIGS))
    m = jnp.maximum(jnp.maximum(l1, l2), l3)
    w1, w2, w3 = jnp.exp(l1 - m), jnp.exp(l2 - m), jnp.exp(l3 - m)
    od = (w1 * o1 + w2 * o2 + w3 * o3) / (w1 + w2 + w3)
    merged = (sgn_ref[...].astype(F32) * _dot(ona_ref[...], wna_ref[...])
              + sgd_ref[...].astype(F32) * _dot(od.astype(BF16), wdil_ref[...]))
    x1 = x_ref[...] + _dot(merged.astype(BF16), wout_ref[...])
    x1_ref[...] = x1
    h2_ref[...] = _rms(x1, g_ref[...])


def _merge(x2d, ona, os_, ls_, sgn, sgd, wna, wdil, wout, gain, tm=512):
    T, D = x2d.shape
    row = lambda w: pl.BlockSpec((tm, w), lambda i: (i, 0))
    residue = [pl.BlockSpec((tm // d, d * DIL_GROUP_WIDTH), lambda i: (i, 0)) for _, d in DIL_CONFIGS]
    n_relayout = 2 * sum(d > 1 for _, d in DIL_CONFIGS)
    return pl.pallas_call(
        _merge_kernel,
        grid=(T // tm,),
        in_specs=[row(D), row(ona.shape[1])] + residue * 2 + [row(D), row(D),
                  _const_spec(wna.shape), _const_spec(wdil.shape), _const_spec(wout.shape), _const_spec((1, D))],
        out_specs=[row(D), row(D)],
        out_shape=[jax.ShapeDtypeStruct((T, D), F32)] * 2,
        scratch_shapes=[_relayout_scratch(tm)] * n_relayout,
        compiler_params=_cparams(1),
        name="merge",
    )(x2d, ona, *os_, *ls_, sgn, sgd, wna, wdil, wout, gain)


def _topk_rows(s, k, payload=None):
    n = s.shape[0]
    row = lax.broadcasted_iota(jnp.int32, s.shape, 0)
    vals, idxs = [], []
    for _ in range(k):
        m = jnp.max(s, axis=0, keepdims=True)
        idx = jnp.min(jnp.where(s == m, row, n), axis=0, keepdims=True)
        sel = row == idx
        vals.append(m)
        idxs.append(idx if payload is None else jnp.max(jnp.where(sel, payload, -1), axis=0, keepdims=True))
        s = jnp.where(sel, -jnp.inf, s)
    return jnp.concatenate(vals, axis=0), jnp.concatenate(idxs, axis=0)


def _exact_bf16_pieces(x):
    p0 = x.astype(BF16)
    r1 = x - p0.astype(F32)
    p1 = r1.astype(BF16)
    p2 = (r1 - p1.astype(F32)).astype(BF16)
    return p0, p1, p2


def _peer_topk_kernel(h_ref, wq_ref, k1_ref, k2_ref, spread_ref, idx_ref, gates_ref, e_scr, g_scr):
    q = _dot(h_ref[...].astype(BF16), wq_ref[...])
    half = PEER_QDIM // 2
    s1 = _dot_nt(k1_ref[...], q[:, :half].astype(BF16))
    s2 = _dot_nt(k2_ref[...], q[:, half:].astype(BF16))
    v1, i1 = _topk_rows(s1, PEER_TOPK)
    v2, i2 = _topk_rows(s2, PEER_TOPK)
    keep = [PEER_TOPK // (i + 1) for i in range(PEER_TOPK)]
    pad = -sum(keep) % SUBLANES
    tm = v1.shape[1]
    cand = jnp.concatenate([v1[i:i + 1] + v2[:n] for i, n in enumerate(keep)]
                           + [jnp.full((pad, tm), -jnp.inf, F32)], axis=0)
    cidx = jnp.concatenate([i1[i:i + 1] * PEER_NKEYS + i2[:n] for i, n in enumerate(keep)]
                           + [jnp.full((pad, tm), -1, jnp.int32)], axis=0)
    sc, e = _topk_rows(cand, PEER_TOPK, payload=cidx)
    p = jnp.exp(sc - sc[0:1])
    head = pl.program_id(1)
    rows = pl.ds(pl.multiple_of(head * PEER_TOPK, PEER_TOPK), PEER_TOPK)
    e_scr[rows, :] = e
    g_scr[rows, :] = p / jnp.sum(p, axis=0, keepdims=True)

    @pl.when(head == PEER_HEADS - 1)
    def _():
        idx_ref[...] = e_scr[...].T * WORD_ROWS
        gates_ref[...] = sum(_dot(piece, spread_ref[...]) for piece in _exact_bf16_pieces(g_scr[...].T))


def _peer_topk(h2, wq, k1, k2, tm=512):
    T, D = h2.shape
    spread = np.zeros((PEER_SLOTS, 2 * PEER_SLOTS), np.float32)
    spread[np.arange(PEER_SLOTS), 2 * np.arange(PEER_SLOTS)] = 1.0
    spread = jnp.asarray(spread, BF16)
    return pl.pallas_call(
        _peer_topk_kernel,
        grid=(T // tm, PEER_HEADS),
        in_specs=[pl.BlockSpec((tm, D), lambda i, h: (i, 0)), pl.BlockSpec((D, PEER_QDIM), lambda i, h: (0, h)),
                  _const_spec(k1.shape), _const_spec(k2.shape), _const_spec(spread.shape)],
        out_specs=[pl.BlockSpec((tm, PEER_SLOTS), lambda i, h: (i, 0)),
                   pl.BlockSpec((tm, 2 * PEER_SLOTS), lambda i, h: (i, 0))],
        out_shape=[jax.ShapeDtypeStruct((T, PEER_SLOTS), jnp.int32), jax.ShapeDtypeStruct((T, 2 * PEER_SLOTS), F32)],
        scratch_shapes=[pltpu.VMEM((PEER_SLOTS, tm), jnp.int32), pltpu.VMEM((PEER_SLOTS, tm), F32)],
        compiler_params=_cparams(2),
        name="peer_topk",
    )(h2, wq, k1, k2, spread)


WORD_ROWS = 4
GATHER_PITCH = 136
HALF = 512


def _pack_table(w):
    e, d = w.shape
    wb = w.astype(BF16)
    pairs = jnp.stack([wb[:, :HALF], wb[:, HALF:]], axis=-1)
    return lax.bitcast_convert_type(pairs, jnp.uint32).reshape(e * WORD_ROWS, LANES)


INDEX_LOOKAHEAD = 8


def _gather_rows(idx_ref, t, tab_ref, gbuf):
    starts = [idx_ref[t, j] for j in range(INDEX_LOOKAHEAD)]
    for j in range(PEER_SLOTS):
        if j + INDEX_LOOKAHEAD < PEER_SLOTS:
            starts.append(idx_ref[t, j + INDEX_LOOKAHEAD])
        row0 = pl.multiple_of(starts[j], WORD_ROWS)
        gbuf[pl.ds(j, WORD_ROWS, stride=GATHER_PITCH), :] = tab_ref[pl.ds(row0, WORD_ROWS), :]


class _StagedIndices:
    def __init__(self, idx_hbm, bufs, sems):
        self.hbm, self.bufs, self.sems = idx_hbm, bufs, sems
        self.half = bufs[0].shape[0]
        self.step = pl.program_id(0)
        self.last = pl.num_programs(0) - 1

        @pl.when(self.step == 0)
        def _():
            for k in range(2):
                self._copy(self.step, k).start()

    def _copy(self, step, k):
        rows = pl.ds((2 * step + k) * self.half, self.half)
        return pltpu.make_async_copy(self.hbm.at[rows], self.bufs[k], self.sems.at[k])

    def gather(self, t, tab_ref, gbuf):
        k, local = divmod(t, self.half)
        if local == 0:
            self._copy(self.step, k).wait()
        _gather_rows(self.bufs[k], local, tab_ref, gbuf)
        if local == self.half - 1:
            self._copy(jnp.minimum(self.step + 1, self.last), k).start()

    def finish(self):
        @pl.when(self.step == self.last)
        def _():
            for k in range(2):
                self._copy(self.last, k).wait()


def _gathered_matrix(gbuf):
    planes = [pltpu.bitcast(gbuf[i * GATHER_PITCH:i * GATHER_PITCH + PEER_SLOTS, :], BF16) for i in range(WORD_ROWS)]
    return jnp.concatenate(planes, axis=1)


def _lhs16(rows, width):
    r = lax.broadcasted_iota(jnp.int32, (16, width), 0)
    out = jnp.zeros((16, width), F32)
    for k, v in enumerate(rows):
        out = jnp.where(r == k, v, out)
    return out.astype(BF16)


PEER_TOKENS_PER_STEP = 64
PIPELINE_LAG = 2


def _pipelined_tokens(tb, gather, compute, bufs):
    n = len(bufs)
    for t in range(tb + PIPELINE_LAG):
        if t < tb:
            gather(t, bufs[t % n])
        if t >= PIPELINE_LAG:
            compute(t - PIPELINE_LAG, bufs[(t - PIPELINE_LAG) % n])


def _peer_u_kernel(idx_hbm, h_ref, tab_ref, g_ref, act_ref, hhi_ref, hlo_ref, c_ref, ga0, ga1, gb0, gb1,
                   idx_a, idx_b, idx_sems, *, tb):
    h = h_ref[...]
    hi = h.astype(BF16).astype(F32)
    hhi_ref[...] = hi
    hlo_ref[...] = h - hi
    even = (lax.broadcasted_iota(jnp.int32, (SUBLANES, 2 * PEER_SLOTS), 1) % 2) == 0
    indices = _StagedIndices(idx_hbm, (idx_a, idx_b), idx_sems)

    def compute(t, gbuf):
        row = pl.ds(t, 1)
        lhs = _lhs16([hhi_ref[row, :HALF], hhi_ref[row, HALF:], hlo_ref[row, :HALF], hlo_ref[row, HALF:]], HALF)
        res = _dot_nt(lhs, _gathered_matrix(gbuf))[:SUBLANES]
        s = res + pltpu.roll(res, 6, 0)
        c_ref[row, :] = jnp.where(even, s, pltpu.roll(s, 7, 0))[0:1]

    _pipelined_tokens(tb, lambda t, g: indices.gather(t, tab_ref, g), compute, (ga0, ga1, gb0, gb1))

    c = c_ref[...]
    a = c + pltpu.roll(c, 2 * PEER_SLOTS - 1, 1)
    gelu = 0.5 * a * (1.0 + lax.erf(a * np.float32(np.sqrt(0.5))))
    lane_even = (lax.broadcasted_iota(jnp.int32, a.shape, 1) % 2) == 0
    act_ref[...] = jnp.where(lane_even, gelu * g_ref[...], 0.0)
    indices.finish()


def _gather_scratch(tb):
    return ([pltpu.VMEM((WORD_ROWS * GATHER_PITCH, LANES), jnp.uint32)] * 4
            + [pltpu.SMEM((tb // 2, PEER_SLOTS), jnp.int32)] * 2 + [pltpu.SemaphoreType.DMA((2,))])


def _peer_u(idx, h2, tab, gates, tb=PEER_TOKENS_PER_STEP):
    T, D = h2.shape
    wide = pl.BlockSpec((tb, 2 * PEER_SLOTS), lambda i: (i, 0))
    return pl.pallas_call(
        functools.partial(_peer_u_kernel, tb=tb),
        grid=(T // tb,),
        in_specs=[pl.BlockSpec(memory_space=pl.ANY),
                  pl.BlockSpec((tb, D), lambda i: (i, 0)), _const_spec(tab.shape), wide],
        out_specs=wide,
        out_shape=jax.ShapeDtypeStruct((T, 2 * PEER_SLOTS), F32),
        scratch_shapes=[pltpu.VMEM((tb, D), F32)] * 2 + [pltpu.VMEM((tb, 2 * PEER_SLOTS), F32)] + _gather_scratch(tb),
        compiler_params=_cparams(1),
        name="peer_u",
    )(idx, h2, tab, gates)


def _peer_v_kernel(idx_hbm, act_ref, x_ref, tab_ref, o_ref, ahi_ref, alo_ref, bhi_ref, blo_ref, acc_ref,
                   ga0, ga1, gb0, gb1, idx_a, idx_b, idx_sems, *, tb):
    a = act_ref[...]
    b = pltpu.roll(a, 1, 1)
    for src, hi_ref, lo_ref in ((a, ahi_ref, alo_ref), (b, bhi_ref, blo_ref)):
        hi = src.astype(BF16).astype(F32)
        hi_ref[...] = hi
        lo_ref[...] = src - hi
    indices = _StagedIndices(idx_hbm, (idx_a, idx_b), idx_sems)

    def compute(t, gbuf):
        row = pl.ds(t, 1)
        lhs = _lhs16([ahi_ref[row, :], bhi_ref[row, :], alo_ref[row, :], blo_ref[row, :]], 2 * PEER_SLOTS)
        res = _dot(lhs, _gathered_matrix(gbuf))[:SUBLANES]
        s = res + pltpu.roll(res, 6, 0)
        acc_ref[row, :HALF] = s[0:1]
        acc_ref[row, HALF:] = s[1:2]

    _pipelined_tokens(tb, lambda t, g: indices.gather(t, tab_ref, g), compute, (ga0, ga1, gb0, gb1))
    o_ref[...] = x_ref[...] + acc_ref[...]
    indices.finish()


def _peer_v(idx, act, x2d, tab, tb=PEER_TOKENS_PER_STEP):
    T, D = x2d.shape
    tok = pl.BlockSpec((tb, D), lambda i: (i, 0))
    wide = pl.BlockSpec((tb, 2 * PEER_SLOTS), lambda i: (i, 0))
    return pl.pallas_call(
        functools.partial(_peer_v_kernel, tb=tb),
        grid=(T // tb,),
        in_specs=[pl.BlockSpec(memory_space=pl.ANY), wide, tok, _const_spec(tab.shape)],
        out_specs=tok,
        out_shape=jax.ShapeDtypeStruct((T, D), F32),
        scratch_shapes=[pltpu.VMEM((tb, 2 * PEER_SLOTS), F32)] * 4 + [pltpu.VMEM((tb, D), F32)] + _gather_scratch(tb),
        compiler_params=_cparams(1),
        name="peer_v",
    )(idx, act, x2d, tab)


def _ple_kernel(x_ref, p_ref, g_ref, wg_ref, wp_ref, o_ref):
    x = x_ref[...]
    h = _rms(x, g_ref[...]).astype(BF16)
    o_ref[...] = x + jax.nn.sigmoid(_dot(h, wg_ref[...])) * _dot(p_ref[...].astype(BF16), wp_ref[...])


def _ple(x2d, p2d, gain, wg, wp, tm=512):
    T, D = x2d.shape
    row = lambda w: pl.BlockSpec((tm, w), lambda i: (i, 0))
    return pl.pallas_call(
        _ple_kernel,
        grid=(T // tm,),
        in_specs=[row(D), row(p2d.shape[1]), _const_spec((1, D)), _const_spec(wg.shape), _const_spec(wp.shape)],
        out_specs=row(D),
        out_shape=jax.ShapeDtypeStruct((T, D), F32),
        compiler_params=_cparams(1),
        name="ple",
    )(x2d, p2d, gain, wg, wp)


def _rotary_tables(seq):
    half = ROT_DIM // 2
    inv_freq = jnp.power(jnp.float32(ROPE_THETA), -jnp.arange(half, dtype=F32) * 2.0 / ROT_DIM)
    ang = jnp.arange(seq).astype(F32)[:, None] * inv_freq[None, :]
    cos, sin = jnp.cos(ang), jnp.sin(ang)
    pad = HEAD_DIM - ROT_DIM
    ra = jnp.concatenate([cos, cos, jnp.ones((seq, pad), F32)], axis=1)
    rb = jnp.concatenate([-sin, sin, jnp.zeros((seq, pad), F32)], axis=1)
    return jnp.tile(ra, (1, DIL_HEADS_PER_GROUP)), jnp.tile(rb, (1, DIL_HEADS_PER_GROUP))


def kernel(x, p, norm_mix, w_in, qk_norm_na, na_rel_bias, qk_norm_dil, w_branch_na, w_branch_dil, w_out, norm_ffn,
           peer_w_query, peer_sub_keys, peer_expert_u, peer_expert_v, norm_ple, w_ple_gate, w_ple):
    B, S, D = x.shape
    T = B * S
    depth = w_in.shape[0]
    wna = NA_HEADS * HEAD_DIM
    wdil = len(DIL_CONFIGS) * DIL_GROUP_WIDTH
    rot_a, rot_b = _rotary_tables(S)
    x2d = x.reshape(T, D)
    for i in range(depth):
        wi = w_in[i].astype(BF16)
        nna = jnp.tile(qk_norm_na[i], (1, NA_HEADS))
        ndil = jnp.tile(qk_norm_dil[i], (1, DIL_HEADS_PER_GROUP))
        (qa, ka, va, q1, k1, v1, q2, k2, v2, q3, k3, v3, sgn, sgd) = _in_proj(
            x2d, norm_mix[i][None], wi[:, :3 * wna], wi[:, 3 * wna:3 * wna + 3 * wdil], wi[:, 3 * wna + 3 * wdil:],
            nna, ndil, rot_a, rot_b, S)

        ona = _na_attn(qa, ka, va, _na_bias_table(na_rel_bias[i], S // GRID_W), B, S)
        dil = [_dil_attn(q, k, v, B, S, window, dilation)
               for (q, k, v), (window, dilation) in zip(((q1, k1, v1), (q2, k2, v2), (q3, k3, v3)), DIL_CONFIGS)]

        x1, h2 = _merge(x2d, ona, [o for o, _ in dil], [l for _, l in dil], sgn, sgd,
                        w_branch_na[i].astype(BF16), w_branch_dil[i].astype(BF16), w_out[i].astype(BF16),
                        norm_ffn[i][None])

        idx, gates = _peer_topk(h2, peer_w_query[i].astype(BF16), peer_sub_keys[i, 0].astype(BF16),
                                peer_sub_keys[i, 1].astype(BF16))
        act = _peer_u(idx, h2, _pack_table(peer_expert_u[i]), gates)
        x2 = _peer_v(idx, act, x1, _pack_table(peer_expert_v[i]))

        x2d = _ple(x2, p[i].reshape(T, -1), norm_ple[i][None], w_ple_gate[i].astype(BF16), w_ple[i].astype(BF16))
    return x2d.reshape(B, S, D)
```

```python
import functools

import numpy as np
import jax
import jax.numpy as jnp
from jax import lax
from jax.experimental import pallas as pl
from jax.experimental.pallas import tpu as pltpu

F32 = jnp.float32
BF16 = jnp.bfloat16

HEAD_DIM = 64
GRID_W = 64
NA_HEADS = 8
NA_WIN_ROWS = 8
NA_WIN_COLS = 16
DIL_CONFIGS = ((128, 1), (512, 4), (2048, 16))
DIL_HEADS_PER_GROUP = 4
DIL_GROUP_WIDTH = DIL_HEADS_PER_GROUP * HEAD_DIM
ROT_DIM = HEAD_DIM // 4
ROPE_THETA = 500000.0
PEER_HEADS = 8
PEER_NKEYS = 128
PEER_QDIM = 256
PEER_TOPK = 16
PEER_SLOTS = PEER_HEADS * PEER_TOPK
RMS_EPS = 1e-6
NEG_BIG = -1e30

LANES = 128
SUBLANES = 8
VMEM_LIMIT_BYTES = 48 * 1024 * 1024


def _cparams(n_axes):
    return pltpu.CompilerParams(dimension_semantics=("arbitrary",) * n_axes, vmem_limit_bytes=VMEM_LIMIT_BYTES)


def _const_spec(shape):
    nd = len(shape)
    return pl.BlockSpec(shape, lambda *_: (0,) * nd, pipeline_mode=pl.Buffered(1))


def _rms(x, gain):
    return x * lax.rsqrt(jnp.mean(x * x, axis=-1, keepdims=True) + RMS_EPS) * gain


def _dot(a, b):
    return jnp.dot(a, b, preferred_element_type=F32)


def _dot_nt(a, b):
    return lax.dot_general(a, b, (((1,), (1,)), ((), ())), preferred_element_type=F32)


def _relayout_scratch(tm):
    return pltpu.VMEM((DIL_GROUP_WIDTH // LANES, tm, LANES), F32)


def _to_residue_layout(x, scr_ref, d):
    tm = x.shape[0]
    slabs = DIL_GROUP_WIDTH // LANES
    for c in range(slabs):
        scr_ref[c] = x[:, c * LANES:(c + 1) * LANES]
    return jnp.concatenate([scr_ref[c, pl.ds(r, tm // d, stride=d), :] for r in range(d) for c in range(slabs)], axis=1)


def _from_residue_layout(y, scr_ref, d):
    rows = y.shape[0]
    slabs = DIL_GROUP_WIDTH // LANES
    for r in range(d):
        for c in range(slabs):
            lo = r * DIL_GROUP_WIDTH + c * LANES
            scr_ref[c, pl.ds(r, rows, stride=d), :] = y[:, lo:lo + LANES]
    return jnp.concatenate([scr_ref[c] for c in range(slabs)], axis=1)


def _head_rms(q, bd_ref, gain):
    outs = []
    for c in range(q.shape[1] // 256):
        qc = q[:, c * 256:(c + 1) * 256]
        sq = qc * qc
        hi = sq.astype(BF16)
        lo = (sq - hi.astype(F32)).astype(BF16)
        ssq = _dot(hi, bd_ref[...]) + _dot(lo, bd_ref[...])
        outs.append(qc * lax.rsqrt(ssq * (1.0 / HEAD_DIM) + RMS_EPS))
    return jnp.concatenate(outs, axis=1) * gain


def _rotary(q, ra, rb):
    lane = lax.broadcasted_iota(jnp.int32, q.shape, 1) % HEAD_DIM
    partner = jnp.where(lane < ROT_DIM // 2, pltpu.roll(q, 256 - ROT_DIM // 2, 1), pltpu.roll(q, ROT_DIM // 2, 1))
    return q * ra + partner * rb


def _in_proj_kernel(x_ref, g_ref, wna_ref, wdil_ref, wgate_ref, nna_ref, ndil_ref, ra_ref, rb_ref, bd_ref,
                    qa_ref, ka_ref, va_ref,
                    q1_ref, k1_ref, v1_ref, q2_ref, k2_ref, v2_ref, q3_ref, k3_ref, v3_ref,
                    sgn_ref, sgd_ref, *relayout_scratch):
    scratch = iter(relayout_scratch)
    h = _rms(x_ref[...], g_ref[...]).astype(BF16)
    scale = HEAD_DIM ** -0.5

    na = _dot(h, wna_ref[...])
    wna = NA_HEADS * HEAD_DIM
    qa_ref[...] = (_head_rms(na[:, :wna], bd_ref, nna_ref[0:1, :]) * scale).astype(BF16)
    ka_ref[...] = _head_rms(na[:, wna:2 * wna], bd_ref, nna_ref[1:2, :]).astype(BF16)
    va_ref[...] = na[:, 2 * wna:].astype(BF16)

    dil = _dot(h, wdil_ref[...])
    wd = len(DIL_CONFIGS) * DIL_GROUP_WIDTH
    ra = ra_ref[...]
    rb = rb_ref[...]
    q_refs = (q1_ref, q2_ref, q3_ref)
    k_refs = (k1_ref, k2_ref, k3_ref)
    v_refs = (v1_ref, v2_ref, v3_ref)
    for g in range(len(DIL_CONFIGS)):
        lo = g * DIL_GROUP_WIDTH
        hi = lo + DIL_GROUP_WIDTH
        q = _head_rms(dil[:, lo:hi], bd_ref, ndil_ref[0:1, :])
        k = _head_rms(dil[:, wd + lo:wd + hi], bd_ref, ndil_ref[1:2, :])
        dilation = DIL_CONFIGS[g][1]
        for ref, val in ((q_refs[g], _rotary(q, ra, rb) * scale), (k_refs[g], _rotary(k, ra, rb)),
                         (v_refs[g], dil[:, 2 * wd + lo:2 * wd + hi])):
            if dilation > 1:
                val = _to_residue_layout(val, next(scratch), dilation)
            ref[...] = val.astype(BF16)

    gate = _dot(h, wgate_ref[...])
    d = sgn_ref.shape[1]
    sgn_ref[...] = jax.nn.sigmoid(gate[:, :d]).astype(BF16)
    sgd_ref[...] = jax.nn.sigmoid(gate[:, d:]).astype(BF16)


def _in_proj(x2d, gain, w_na, w_dil, w_gate, nna, ndil, rot_a, rot_b, seq, tm=512):
    T, D = x2d.shape
    bd = jnp.asarray(np.kron(np.eye(256 // HEAD_DIM), np.ones((HEAD_DIM, HEAD_DIM))), BF16)
    wna = NA_HEADS * HEAD_DIM
    nseq = seq // tm
    row = lambda w: pl.BlockSpec((tm, w), lambda i: (i, 0))
    rot = pl.BlockSpec((tm, DIL_GROUP_WIDTH), lambda i: (i % nseq, 0))
    dil_shapes = [(T // d, d * DIL_GROUP_WIDTH) for _, d in DIL_CONFIGS for _ in range(3)]
    dil_specs = [pl.BlockSpec((tm // d, d * DIL_GROUP_WIDTH), lambda i: (i, 0)) for _, d in DIL_CONFIGS for _ in range(3)]
    outs = [jax.ShapeDtypeStruct((T, wna), BF16)] * 3 + [jax.ShapeDtypeStruct(s, BF16) for s in dil_shapes] \
        + [jax.ShapeDtypeStruct((T, D), BF16)] * 2
    n_relayout = 3 * sum(d > 1 for _, d in DIL_CONFIGS)
    return pl.pallas_call(
        _in_proj_kernel,
        grid=(T // tm,),
        in_specs=[row(D), _const_spec((1, D)), _const_spec(w_na.shape), _const_spec(w_dil.shape),
                  _const_spec(w_gate.shape), _const_spec(nna.shape), _const_spec(ndil.shape), rot, rot,
                  _const_spec(bd.shape)],
        out_specs=[row(wna)] * 3 + dil_specs + [row(D)] * 2,
        out_shape=outs,
        scratch_shapes=[_relayout_scratch(tm)] * n_relayout,
        compiler_params=_cparams(1),
        name="in_proj",
    )(x2d, gain, w_na, w_dil, w_gate, nna, ndil, rot_a, rot_b, bd)


def _na_row_start(r, rows):
    return np.clip(r - NA_WIN_ROWS // 2, 0, rows - NA_WIN_ROWS)


NA_GROUP_ROWS = 4
NA_GROUP_WINDOW = 12


def _na_group_start(g, rows, xp=jnp):
    return xp.clip(g * NA_GROUP_ROWS - NA_WIN_ROWS // 2, 0, rows - NA_GROUP_WINDOW)


def _head_pair_rows(x, first):
    zero = jnp.zeros_like(x)
    return jnp.concatenate([jnp.where(first, x, zero), jnp.where(first, zero, x)], axis=0)


def _na_kernel(q_ref, k_ref, v_ref, b_ref, o_ref, *, rows):
    nq = NA_GROUP_ROWS * GRID_W
    nk = NA_GROUP_WINDOW * GRID_W
    pair = 2 * HEAD_DIM
    first = lax.broadcasted_iota(jnp.int32, (nq, pair), 1) < HEAD_DIM
    off = pl.multiple_of(_na_group_start(pl.program_id(1), rows) * GRID_W, GRID_W)
    kw = k_ref[pl.ds(off, nk), :]
    vw = v_ref[pl.ds(off, nk), :]
    q = q_ref[...]
    outs = []
    for hp in range(NA_HEADS // 2):
        sl = slice(hp * pair, (hp + 1) * pair)
        s = _dot_nt(_head_pair_rows(q[:, sl], first), kw[:, sl]) + b_ref[0, hp]
        m = jnp.max(s, axis=-1, keepdims=True)
        p = jnp.exp(s - m)
        l = jnp.sum(p, axis=-1, keepdims=True)
        o = _dot(p.astype(BF16), vw[:, sl]) / l
        outs.append(jnp.where(first, o[:nq], o[nq:]))
    o_ref[...] = jnp.concatenate(outs, axis=1).astype(BF16)


def _na_bias_table(rpb, rows):
    qc = np.arange(GRID_W)[:, None]
    kc = np.arange(GRID_W)[None, :]
    cs = np.clip(qc - NA_WIN_COLS // 2, 0, GRID_W - NA_WIN_COLS)
    col_ok = (kc >= cs) & (kc < cs + NA_WIN_COLS)
    dc = np.clip(kc - qc + NA_WIN_COLS - 1, 0, 2 * NA_WIN_COLS - 2)
    colb = rpb.astype(F32)[:, :, dc]
    H = rpb.shape[0]
    ngroups = rows // NA_GROUP_ROWS
    assert ngroups >= 3 and rows >= NA_GROUP_WINDOW
    tabs = []
    for g in (0, 1, ngroups - 1):
        key_row = _na_group_start(g, rows, np) + np.arange(NA_GROUP_WINDOW)
        per_row = []
        for r in range(g * NA_GROUP_ROWS, (g + 1) * NA_GROUP_ROWS):
            rs = _na_row_start(r, rows)
            row_ok = (key_row >= rs) & (key_row < rs + NA_WIN_ROWS)
            dr = np.clip(key_row - r + NA_WIN_ROWS - 1, 0, 2 * NA_WIN_ROWS - 2)
            ok = row_ok[:, None, None] & col_ok[None]
            b = jnp.where(ok[None], colb[:, dr], NEG_BIG)
            per_row.append(b.transpose(0, 2, 1, 3).reshape(H, GRID_W, NA_GROUP_WINDOW * GRID_W))
        t = jnp.stack(per_row, axis=1)
        tabs.append(t.reshape(H // 2, 2 * NA_GROUP_ROWS * GRID_W, NA_GROUP_WINDOW * GRID_W))
    return jnp.stack(tabs)


def _na_attn(qa, ka, va, bias_tab, batch, seq):
    T, W = qa.shape
    rows = seq // GRID_W
    ngroups = rows // NA_GROUP_ROWS

    def bias_idx(b, g):
        return (jnp.where(g == 0, 0, jnp.where(g == ngroups - 1, 2, 1)), 0, 0, 0)

    kv = pl.BlockSpec((seq, W), lambda b, g: (b, 0))
    qo = pl.BlockSpec((NA_GROUP_ROWS * GRID_W, W), lambda b, g: (b * ngroups + g, 0))
    return pl.pallas_call(
        functools.partial(_na_kernel, rows=rows),
        grid=(batch, ngroups),
        in_specs=[qo, kv, kv, pl.BlockSpec((1,) + bias_tab.shape[1:], bias_idx)],
        out_specs=qo,
        out_shape=jax.ShapeDtypeStruct((T, W), BF16),
        compiler_params=_cparams(2),
        name="na_attn",
    )(qa, ka, va, bias_tab)


DIL_QUERY_BLOCK = 128
DIL_BLOCKS_PER_ITER = 8


def _dil_kernel(q_ref, k_ref, v_ref, o_ref, lse_ref, *, length, side, qb, tiles):
    kb = qb + 2 * side
    nblk = length // qb
    qi = lax.broadcasted_iota(jnp.int32, (2 * qb, kb), 0) % qb
    kj = lax.broadcasted_iota(jnp.int32, (2 * qb, kb), 1)
    rel = kj - qi
    first = lax.broadcasted_iota(jnp.int32, (qb, LANES), 1) < HEAD_DIM

    def block(tile, i):
        lanes = slice(tile * LANES, (tile + 1) * LANES)
        qs = pl.multiple_of(i * qb, qb)
        ws = pl.multiple_of(jnp.clip(qs - side, 0, length - kb), side)
        delta = rel + (ws - qs)
        band = (delta >= -side) & (delta <= side)
        q = q_ref[0, pl.ds(qs, qb), lanes]
        k = k_ref[0, pl.ds(ws, kb), lanes]
        v = v_ref[0, pl.ds(ws, kb), lanes]
        s = jnp.where(band, _dot_nt(_head_pair_rows(q, first), k), NEG_BIG)
        m = jnp.max(s, axis=-1, keepdims=True)
        p = jnp.exp(s - m)
        l = jnp.sum(p, axis=-1, keepdims=True)
        o = _dot(p.astype(BF16), v) / l
        lse = jnp.broadcast_to(m + jnp.log(l), (2 * qb, LANES))
        o_ref[0, pl.ds(qs, qb), lanes] = jnp.where(first, o[:qb], o[qb:])
        lse_ref[0, pl.ds(qs, qb), lanes] = jnp.where(first, lse[:qb], lse[qb:])

    per_iter = min(DIL_BLOCKS_PER_ITER, nblk)

    def several(it, _):
        for tile in range(tiles):
            for u in range(per_iter):
                block(tile, it * per_iter + u)
        return 0

    lax.fori_loop(0, nblk // per_iter, several, 0)


def _dil_attn(q, k, v, batch, seq, window, dilation):
    length = seq // dilation
    side = window // (2 * dilation)
    qb = min(DIL_QUERY_BLOCK, length - 2 * side)
    width = dilation * DIL_GROUP_WIDTH
    shp = (batch, length, width)
    tiles = max(1, DIL_BLOCKS_PER_ITER // (length // qb))
    spec = pl.BlockSpec((1, length, tiles * LANES), lambda b, j: (b, 0, j))
    o, lse = pl.pallas_call(
        functools.partial(_dil_kernel, length=length, side=side, qb=qb, tiles=tiles),
        grid=(batch, width // (tiles * LANES)),
        in_specs=[spec, spec, spec],
        out_specs=[spec, spec],
        out_shape=[jax.ShapeDtypeStruct(shp, F32)] * 2,
        compiler_params=_cparams(2),
        name=f"dil_attn_d{dilation}",
    )(q.reshape(shp), k.reshape(shp), v.reshape(shp))
    return o.reshape(q.shape), lse.reshape(q.shape)


def _merge_kernel(x_ref, ona_ref, o1_ref, o2_ref, o3_ref, l1_ref, l2_ref, l3_ref, sgn_ref, sgd_ref,
                  wna_ref, wdil_ref, wout_ref, g_ref, x1_ref, h2_ref, *relayout_scratch):
    scratch = iter(relayout_scratch)

    def token_order(ref, dilation):
        return ref[...] if dilation == 1 else _from_residue_layout(ref[...], next(scratch), dilation)

    o1, o2, o3 = (token_order(r, d) for r, (_, d) in zip((o1_ref, o2_ref, o3_ref), DIL_CONFIGS))
    l1, l2, l3 = (token_order(r, d) for r, (_, d) in zip((l1_ref, l2_ref, l3_ref), DIL_CONFIGS))
    m = jnp.maximum(jnp.maximum(l1, l2), l3)
    w1, w2, w3 = jnp.exp(l1 - m), jnp.exp(l2 - m), jnp.exp(l3 - m)
    od = (w1 * o1 + w2 * o2 + w3 * o3) / (w1 + w2 + w3)
    merged = (sgn_ref[...].astype(F32) * _dot(ona_ref[...], wna_ref[...])
              + sgd_ref[...].astype(F32) * _dot(od.astype(BF16), wdil_ref[...]))
    x1 = x_ref[...] + _dot(merged.astype(BF16), wout_ref[...])
    x1_ref[...] = x1
    h2_ref[...] = _rms(x1, g_ref[...])


def _merge(x2d, ona, os_, ls_, sgn, sgd, wna, wdil, wout, gain, tm=512):
    T, D = x2d.shape
    row = lambda w: pl.BlockSpec((tm, w), lambda i: (i, 0))
    residue = [pl.BlockSpec((tm // d, d * DIL_GROUP_WIDTH), lambda i: (i, 0)) for _, d in DIL_CONFIGS]
    n_relayout = 2 * sum(d > 1 for _, d in DIL_CONFIGS)
    return pl.pallas_call(
        _merge_kernel,
        grid=(T // tm,),
        in_specs=[row(D), row(ona.shape[1])] + residue * 2 + [row(D), row(D),
                  _const_spec(wna.shape), _const_spec(wdil.shape), _const_spec(wout.shape), _const_spec((1, D))],
        out_specs=[row(D), row(D)],
        out_shape=[jax.ShapeDtypeStruct((T, D), F32)] * 2,
        scratch_shapes=[_relayout_scratch(tm)] * n_relayout,
        compiler_params=_cparams(1),
        name="merge",
    )(x2d, ona, *os_, *ls_, sgn, sgd, wna, wdil, wout, gain)


def _topk_rows(s, k, payload=None):
    n = s.shape[0]
    row = lax.broadcasted_iota(jnp.int32, s.shape, 0)
    vals, idxs = [], []
    for _ in range(k):
        m = jnp.max(s, axis=0, keepdims=True)
        idx = jnp.min(jnp.where(s == m, row, n), axis=0, keepdims=True)
        sel = row == idx
        vals.append(m)
        idxs.append(idx if payload is None else jnp.max(jnp.where(sel, payload, -1), axis=0, keepdims=True))
        s = jnp.where(sel, -jnp.inf, s)
    return jnp.concatenate(vals, axis=0), jnp.concatenate(idxs, axis=0)


def _exact_bf16_pieces(x):
    p0 = x.astype(BF16)
    r1 = x - p0.astype(F32)
    p1 = r1.astype(BF16)
    p2 = (r1 - p1.astype(F32)).astype(BF16)
    return p0, p1, p2


def _peer_topk_kernel(h_ref, wq_ref, k1_ref, k2_ref, spread_ref, idx_ref, gates_ref, e_scr, g_scr):
    q = _dot(h_ref[...].astype(BF16), wq_ref[...])
    half = PEER_QDIM // 2
    s1 = _dot_nt(k1_ref[...], q[:, :half].astype(BF16))
    s2 = _dot_nt(k2_ref[...], q[:, half:].astype(BF16))
    v1, i1 = _topk_rows(s1, PEER_TOPK)
    v2, i2 = _topk_rows(s2, PEER_TOPK)
    keep = [PEER_TOPK // (i + 1) for i in range(PEER_TOPK)]
    pad = -sum(keep) % SUBLANES
    tm = v1.shape[1]
    cand = jnp.concatenate([v1[i:i + 1] + v2[:n] for i, n in enumerate(keep)]
                           + [jnp.full((pad, tm), -jnp.inf, F32)], axis=0)
    cidx = jnp.concatenate([i1[i:i + 1] * PEER_NKEYS + i2[:n] for i, n in enumerate(keep)]
                           + [jnp.full((pad, tm), -1, jnp.int32)], axis=0)
    sc, e = _topk_rows(cand, PEER_TOPK, payload=cidx)
    p = jnp.exp(sc - sc[0:1])
    head = pl.program_id(1)
    rows = pl.ds(pl.multiple_of(head * PEER_TOPK, PEER_TOPK), PEER_TOPK)
    e_scr[rows, :] = e
    g_scr[rows, :] = p / jnp.sum(p, axis=0, keepdims=True)

    @pl.when(head == PEER_HEADS - 1)
    def _():
        idx_ref[...] = e_scr[...].T * WORD_ROWS
        gates_ref[...] = sum(_dot(piece, spread_ref[...]) for piece in _exact_bf16_pieces(g_scr[...].T))


def _peer_topk(h2, wq, k1, k2, tm=1024):
    T, D = h2.shape
    spread = np.zeros((PEER_SLOTS, 2 * PEER_SLOTS), np.float32)
    spread[np.arange(PEER_SLOTS), 2 * np.arange(PEER_SLOTS)] = 1.0
    spread = jnp.asarray(spread, BF16)
    return pl.pallas_call(
        _peer_topk_kernel,
        grid=(T // tm, PEER_HEADS),
        in_specs=[pl.BlockSpec((tm, D), lambda i, h: (i, 0)), pl.BlockSpec((D, PEER_QDIM), lambda i, h: (0, h)),
                  _const_spec(k1.shape), _const_spec(k2.shape), _const_spec(spread.shape)],
        out_specs=[pl.BlockSpec((tm, PEER_SLOTS), lambda i, h: (i, 0)),
                   pl.BlockSpec((tm, 2 * PEER_SLOTS), lambda i, h: (i, 0))],
        out_shape=[jax.ShapeDtypeStruct((T, PEER_SLOTS), jnp.int32), jax.ShapeDtypeStruct((T, 2 * PEER_SLOTS), F32)],
        scratch_shapes=[pltpu.VMEM((PEER_SLOTS, tm), jnp.int32), pltpu.VMEM((PEER_SLOTS, tm), F32)],
        compiler_params=_cparams(2),
        name="peer_topk",
    )(h2, wq, k1, k2, spread)


WORD_ROWS = 4
GATHER_PITCH = 136
HALF = 512


def _pack_table(w):
    e, d = w.shape
    wb = w.astype(BF16)
    pairs = jnp.stack([wb[:, :HALF], wb[:, HALF:]], axis=-1)
    return lax.bitcast_convert_type(pairs, jnp.uint32).reshape(e * WORD_ROWS, LANES)


INDEX_LOOKAHEAD = 8


def _gather_rows(idx_ref, t, tab_ref, gbuf):
    starts = [idx_ref[t, j] for j in range(INDEX_LOOKAHEAD)]
    for j in range(PEER_SLOTS):
        if j + INDEX_LOOKAHEAD < PEER_SLOTS:
            starts.append(idx_ref[t, j + INDEX_LOOKAHEAD])
        row0 = pl.multiple_of(starts[j], WORD_ROWS)
        gbuf[pl.ds(j, WORD_ROWS, stride=GATHER_PITCH), :] = tab_ref[pl.ds(row0, WORD_ROWS), :]


class _StagedIndices:
    def __init__(self, idx_hbm, bufs, sems):
        self.hbm, self.bufs, self.sems = idx_hbm, bufs, sems
        self.half = bufs[0].shape[0]
        self.step = pl.program_id(0)
        self.last = pl.num_programs(0) - 1

        @pl.when(self.step == 0)
        def _():
            for k in range(2):
                self._copy(self.step, k).start()

    def _copy(self, step, k):
        rows = pl.ds((2 * step + k) * self.half, self.half)
        return pltpu.make_async_copy(self.hbm.at[rows], self.bufs[k], self.sems.at[k])

    def gather(self, t, tab_ref, gbuf):
        k, local = divmod(t, self.half)
        if local == 0:
            self._copy(self.step, k).wait()
        _gather_rows(self.bufs[k], local, tab_ref, gbuf)
        if local == self.half - 1:
            self._copy(jnp.minimum(self.step + 1, self.last), k).start()

    def finish(self):
        @pl.when(self.step == self.last)
        def _():
            for k in range(2):
                self._copy(self.last, k).wait()


def _gathered_matrix(gbuf):
    planes = [pltpu.bitcast(gbuf[i * GATHER_PITCH:i * GATHER_PITCH + PEER_SLOTS, :], BF16) for i in range(WORD_ROWS)]
    return jnp.concatenate(planes, axis=1)


def _lhs16(rows, width):
    r = lax.broadcasted_iota(jnp.int32, (16, width), 0)
    out = jnp.zeros((16, width), F32)
    for k, v in enumerate(rows):
        out = jnp.where(r == k, v, out)
    return out.astype(BF16)


PEER_TOKENS_PER_STEP = 64
PIPELINE_LAG = 2


def _pipelined_tokens(tb, gather, compute, bufs):
    n = len(bufs)
    for t in range(tb + PIPELINE_LAG):
        if t < tb:
            gather(t, bufs[t % n])
        if t >= PIPELINE_LAG:
            compute(t - PIPELINE_LAG, bufs[(t - PIPELINE_LAG) % n])


def _peer_u_kernel(idx_hbm, h_ref, tab_ref, g_ref, act_ref, hhi_ref, hlo_ref, c_ref, ga0, ga1, gb0, gb1,
                   idx_a, idx_b, idx_sems, *, tb):
    h = h_ref[...]
    hi = h.astype(BF16).astype(F32)
    hhi_ref[...] = hi
    hlo_ref[...] = h - hi
    even = (lax.broadcasted_iota(jnp.int32, (SUBLANES, 2 * PEER_SLOTS), 1) % 2) == 0
    indices = _StagedIndices(idx_hbm, (idx_a, idx_b), idx_sems)

    def compute(t, gbuf):
        row = pl.ds(t, 1)
        lhs = _lhs16([hhi_ref[row, :HALF], hhi_ref[row, HALF:], hlo_ref[row, :HALF], hlo_ref[row, HALF:]], HALF)
        res = _dot_nt(lhs, _gathered_matrix(gbuf))[:SUBLANES]
        s = res + pltpu.roll(res, 6, 0)
        c_ref[row, :] = jnp.where(even, s, pltpu.roll(s, 7, 0))[0:1]

    _pipelined_tokens(tb, lambda t, g: indices.gather(t, tab_ref, g), compute, (ga0, ga1, gb0, gb1))

    c = c_ref[...]
    a = c + pltpu.roll(c, 2 * PEER_SLOTS - 1, 1)
    gelu = 0.5 * a * (1.0 + lax.erf(a * np.float32(np.sqrt(0.5))))
    lane_even = (lax.broadcasted_iota(jnp.int32, a.shape, 1) % 2) == 0
    act_ref[...] = jnp.where(lane_even, gelu * g_ref[...], 0.0)
    indices.finish()


def _gather_scratch(tb):
    return ([pltpu.VMEM((WORD_ROWS * GATHER_PITCH, LANES), jnp.uint32)] * 4
            + [pltpu.SMEM((tb // 2, PEER_SLOTS), jnp.int32)] * 2 + [pltpu.SemaphoreType.DMA((2,))])


def _peer_u(idx, h2, tab, gates, tb=PEER_TOKENS_PER_STEP):
    T, D = h2.shape
    wide = pl.BlockSpec((tb, 2 * PEER_SLOTS), lambda i: (i, 0))
    return pl.pallas_call(
        functools.partial(_peer_u_kernel, tb=tb),
        grid=(T // tb,),
        in_specs=[pl.BlockSpec(memory_space=pl.ANY),
                  pl.BlockSpec((tb, D), lambda i: (i, 0)), _const_spec(tab.shape), wide],
        out_specs=wide,
        out_shape=jax.ShapeDtypeStruct((T, 2 * PEER_SLOTS), F32),
        scratch_shapes=[pltpu.VMEM((tb, D), F32)] * 2 + [pltpu.VMEM((tb, 2 * PEER_SLOTS), F32)] + _gather_scratch(tb),
        compiler_params=_cparams(1),
        name="peer_u",
    )(idx, h2, tab, gates)


def _peer_v_kernel(idx_hbm, act_ref, x_ref, tab_ref, o_ref, ahi_ref, alo_ref, bhi_ref, blo_ref, acc_ref,
                   ga0, ga1, gb0, gb1, idx_a, idx_b, idx_sems, *, tb):
    a = act_ref[...]
    b = pltpu.roll(a, 1, 1)
    for src, hi_ref, lo_ref in ((a, ahi_ref, alo_ref), (b, bhi_ref, blo_ref)):
        hi = src.astype(BF16).astype(F32)
        hi_ref[...] = hi
        lo_ref[...] = src - hi
    indices = _StagedIndices(idx_hbm, (idx_a, idx_b), idx_sems)

    def compute(t, gbuf):
        row = pl.ds(t, 1)
        lhs = _lhs16([ahi_ref[row, :], bhi_ref[row, :], alo_ref[row, :], blo_ref[row, :]], 2 * PEER_SLOTS)
        res = _dot(lhs, _gathered_matrix(gbuf))[:SUBLANES]
        s = res + pltpu.roll(res, 6, 0)
        acc_ref[row, :HALF] = s[0:1]
        acc_ref[row, HALF:] = s[1:2]

    _pipelined_tokens(tb, lambda t, g: indices.gather(t, tab_ref, g), compute, (ga0, ga1, gb0, gb1))
    o_ref[...] = x_ref[...] + acc_ref[...]
    indices.finish()


def _peer_v(idx, act, x2d, tab, tb=PEER_TOKENS_PER_STEP):
    T, D = x2d.shape
    tok = pl.BlockSpec((tb, D), lambda i: (i, 0))
    wide = pl.BlockSpec((tb, 2 * PEER_SLOTS), lambda i: (i, 0))
    return pl.pallas_call(
        functools.partial(_peer_v_kernel, tb=tb),
        grid=(T // tb,),
        in_specs=[pl.BlockSpec(memory_space=pl.ANY), wide, tok, _const_spec(tab.shape)],
        out_specs=tok,
        out_shape=jax.ShapeDtypeStruct((T, D), F32),
        scratch_shapes=[pltpu.VMEM((tb, 2 * PEER_SLOTS), F32)] * 4 + [pltpu.VMEM((tb, D), F32)] + _gather_scratch(tb),
        compiler_params=_cparams(1),
        name="peer_v",
    )(idx, act, x2d, tab)


def _ple_kernel(x_ref, p_ref, g_ref, wg_ref, wp_ref, o_ref):
    x = x_ref[...]
    h = _rms(x, g_ref[...]).astype(BF16)
    o_ref[...] = x + jax.nn.sigmoid(_dot(h, wg_ref[...])) * _dot(p_ref[...].astype(BF16), wp_ref[...])


def _ple(x2d, p2d, gain, wg, wp, tm=512):
    T, D = x2d.shape
    row = lambda w: pl.BlockSpec((tm, w), lambda i: (i, 0))
    return pl.pallas_call(
        _ple_kernel,
        grid=(T // tm,),
        in_specs=[row(D), row(p2d.shape[1]), _const_spec((1, D)), _const_spec(wg.shape), _const_spec(wp.shape)],
        out_specs=row(D),
        out_shape=jax.ShapeDtypeStruct((T, D), F32),
        compiler_params=_cparams(1),
        name="ple",
    )(x2d, p2d, gain, wg, wp)


def _rotary_tables(seq):
    half = ROT_DIM // 2
    inv_freq = jnp.power(jnp.float32(ROPE_THETA), -jnp.arange(half, dtype=F32) * 2.0 / ROT_DIM)
    ang = jnp.arange(seq).astype(F32)[:, None] * inv_freq[None, :]
    cos, sin = jnp.cos(ang), jnp.sin(ang)
    pad = HEAD_DIM - ROT_DIM
    ra = jnp.concatenate([cos, cos, jnp.ones((seq, pad), F32)], axis=1)
    rb = jnp.concatenate([-sin, sin, jnp.zeros((seq, pad), F32)], axis=1)
    return jnp.tile(ra, (1, DIL_HEADS_PER_GROUP)), jnp.tile(rb, (1, DIL_HEADS_PER_GROUP))


def kernel(x, p, norm_mix, w_in, qk_norm_na, na_rel_bias, qk_norm_dil, w_branch_na, w_branch_dil, w_out, norm_ffn,
           peer_w_query, peer_sub_keys, peer_expert_u, peer_expert_v, norm_ple, w_ple_gate, w_ple):
    B, S, D = x.shape
    T = B * S
    depth = w_in.shape[0]
    wna = NA_HEADS * HEAD_DIM
    wdil = len(DIL_CONFIGS) * DIL_GROUP_WIDTH
    rot_a, rot_b = _rotary_tables(S)
    x2d = x.reshape(T, D)
    for i in range(depth):
        wi = w_in[i].astype(BF16)
        nna = jnp.tile(qk_norm_na[i], (1, NA_HEADS))
        ndil = jnp.tile(qk_norm_dil[i], (1, DIL_HEADS_PER_GROUP))
        (qa, ka, va, q1, k1, v1, q2, k2, v2, q3, k3, v3, sgn, sgd) = _in_proj(
            x2d, norm_mix[i][None], wi[:, :3 * wna], wi[:, 3 * wna:3 * wna + 3 * wdil], wi[:, 3 * wna + 3 * wdil:],
            nna, ndil, rot_a, rot_b, S)

        ona = _na_attn(qa, ka, va, _na_bias_table(na_rel_bias[i], S // GRID_W), B, S)
        dil = [_dil_attn(q, k, v, B, S, window, dilation)
               for (q, k, v), (window, dilation) in zip(((q1, k1, v1), (q2, k2, v2), (q3, k3, v3)), DIL_CONFIGS)]

        x1, h2 = _merge(x2d, ona, [o for o, _ in dil], [l for _, l in dil], sgn, sgd,
                        w_branch_na[i].astype(BF16), w_branch_dil[i].astype(BF16), w_out[i].astype(BF16),
                        norm_ffn[i][None])

        idx, gates = _peer_topk(h2, peer_w_query[i].astype(BF16), peer_sub_keys[i, 0].astype(BF16),
                                peer_sub_keys[i, 1].astype(BF16))
        act = _peer_u(idx, h2, _pack_table(peer_expert_u[i]), gates)
        x2 = _peer_v(idx, act, x1, _pack_table(peer_expert_v[i]))

        x2d = _ple(x2, p[i].reshape(T, -1), norm_ple[i][None], w_ple_gate[i].astype(BF16), w_ple[i].astype(BF16))
    return x2d.reshape(B, S, D)
```

```python
import functools

import numpy as np
import jax
import jax.numpy as jnp
from jax import lax
from jax.experimental import pallas as pl
from jax.experimental.pallas import tpu as pltpu

F32 = jnp.float32
BF16 = jnp.bfloat16

HEAD_DIM = 64
GRID_W = 64
NA_HEADS = 8
NA_WIN_ROWS = 8
NA_WIN_COLS = 16
DIL_CONFIGS = ((128, 1), (512, 4), (2048, 16))
DIL_HEADS_PER_GROUP = 4
DIL_GROUP_WIDTH = DIL_HEADS_PER_GROUP * HEAD_DIM
ROT_DIM = HEAD_DIM // 4
ROPE_THETA = 500000.0
PEER_HEADS = 8
PEER_NKEYS = 128
PEER_QDIM = 256
PEER_TOPK = 16
PEER_SLOTS = PEER_HEADS * PEER_TOPK
RMS_EPS = 1e-6
NEG_BIG = -1e30

LANES = 128
SUBLANES = 8
VMEM_LIMIT_BYTES = 48 * 1024 * 1024


def _cparams(n_axes):
    return pltpu.CompilerParams(dimension_semantics=("arbitrary",) * n_axes, vmem_limit_bytes=VMEM_LIMIT_BYTES)


def _const_spec(shape):
    nd = len(shape)
    return pl.BlockSpec(shape, lambda *_: (0,) * nd, pipeline_mode=pl.Buffered(1))


def _rms(x, gain):
    return x * lax.rsqrt(jnp.mean(x * x, axis=-1, keepdims=True) + RMS_EPS) * gain


def _dot(a, b):
    return jnp.dot(a, b, preferred_element_type=F32)


def _dot_nt(a, b):
    return lax.dot_general(a, b, (((1,), (1,)), ((), ())), preferred_element_type=F32)


def _relayout_scratch(tm):
    return pltpu.VMEM((DIL_GROUP_WIDTH // LANES, tm, LANES), F32)


def _to_residue_layout(x, scr_ref, d):
    tm = x.shape[0]
    slabs = DIL_GROUP_WIDTH // LANES
    for c in range(slabs):
        scr_ref[c] = x[:, c * LANES:(c + 1) * LANES]
    return jnp.concatenate([scr_ref[c, pl.ds(r, tm // d, stride=d), :] for r in range(d) for c in range(slabs)], axis=1)


def _from_residue_layout(y, scr_ref, d):
    rows = y.shape[0]
    slabs = DIL_GROUP_WIDTH // LANES
    for r in range(d):
        for c in range(slabs):
            lo = r * DIL_GROUP_WIDTH + c * LANES
            scr_ref[c, pl.ds(r, rows, stride=d), :] = y[:, lo:lo + LANES]
    return jnp.concatenate([scr_ref[c] for c in range(slabs)], axis=1)


def _head_rms(q, bd_ref, gain):
    outs = []
    for c in range(q.shape[1] // 256):
        qc = q[:, c * 256:(c + 1) * 256]
        sq = qc * qc
        hi = sq.astype(BF16)
        lo = (sq - hi.astype(F32)).astype(BF16)
        ssq = _dot(hi, bd_ref[...]) + _dot(lo, bd_ref[...])
        outs.append(qc * lax.rsqrt(ssq * (1.0 / HEAD_DIM) + RMS_EPS))
    return jnp.concatenate(outs, axis=1) * gain


def _rotary(q, ra, rb):
    lane = lax.broadcasted_iota(jnp.int32, q.shape, 1) % HEAD_DIM
    partner = jnp.where(lane < ROT_DIM // 2, pltpu.roll(q, 256 - ROT_DIM // 2, 1), pltpu.roll(q, ROT_DIM // 2, 1))
    return q * ra + partner * rb


def _in_proj_kernel(x_ref, g_ref, wna_ref, wdil_ref, wgate_ref, nna_ref, ndil_ref, ra_ref, rb_ref, bd_ref,
                    qa_ref, ka_ref, va_ref,
                    q1_ref, k1_ref, v1_ref, q2_ref, k2_ref, v2_ref, q3_ref, k3_ref, v3_ref,
                    sgn_ref, sgd_ref, *relayout_scratch):
    scratch = iter(relayout_scratch)
    h = _rms(x_ref[...], g_ref[...]).astype(BF16)
    scale = HEAD_DIM ** -0.5

    na = _dot(h, wna_ref[...])
    wna = NA_HEADS * HEAD_DIM
    qa_ref[...] = (_head_rms(na[:, :wna], bd_ref, nna_ref[0:1, :]) * scale).astype(BF16)
    ka_ref[...] = _head_rms(na[:, wna:2 * wna], bd_ref, nna_ref[1:2, :]).astype(BF16)
    va_ref[...] = na[:, 2 * wna:].astype(BF16)

    dil = _dot(h, wdil_ref[...])
    wd = len(DIL_CONFIGS) * DIL_GROUP_WIDTH
    ra = ra_ref[...]
    rb = rb_ref[...]
    q_refs = (q1_ref, q2_ref, q3_ref)
    k_refs = (k1_ref, k2_ref, k3_ref)
    v_refs = (v1_ref, v2_ref, v3_ref)
    for g in range(len(DIL_CONFIGS)):
        lo = g * DIL_GROUP_WIDTH
        hi = lo + DIL_GROUP_WIDTH
        q = _head_rms(dil[:, lo:hi], bd_ref, ndil_ref[0:1, :])
        k = _head_rms(dil[:, wd + lo:wd + hi], bd_ref, ndil_ref[1:2, :])
        dilation = DIL_CONFIGS[g][1]
        for ref, val in ((q_refs[g], _rotary(q, ra, rb) * scale), (k_refs[g], _rotary(k, ra, rb)),
                         (v_refs[g], dil[:, 2 * wd + lo:2 * wd + hi])):
            if dilation > 1:
                val = _to_residue_layout(val, next(scratch), dilation)
            ref[...] = val.astype(BF16)

    gate = _dot(h, wgate_ref[...])
    d = sgn_ref.shape[1]
    sgn_ref[...] = jax.nn.sigmoid(gate[:, :d]).astype(BF16)
    sgd_ref[...] = jax.nn.sigmoid(gate[:, d:]).astype(BF16)


def _in_proj(x2d, gain, w_na, w_dil, w_gate, nna, ndil, rot_a, rot_b, seq, tm=512):
    T, D = x2d.shape
    bd = jnp.asarray(np.kron(np.eye(256 // HEAD_DIM), np.ones((HEAD_DIM, HEAD_DIM))), BF16)
    wna = NA_HEADS * HEAD_DIM
    nseq = seq // tm
    row = lambda w: pl.BlockSpec((tm, w), lambda i: (i, 0))
    rot = pl.BlockSpec((tm, DIL_GROUP_WIDTH), lambda i: (i % nseq, 0))
    dil_shapes = [(T // d, d * DIL_GROUP_WIDTH) for _, d in DIL_CONFIGS for _ in range(3)]
    dil_specs = [pl.BlockSpec((tm // d, d * DIL_GROUP_WIDTH), lambda i: (i, 0)) for _, d in DIL_CONFIGS for _ in range(3)]
    outs = [jax.ShapeDtypeStruct((T, wna), BF16)] * 3 + [jax.ShapeDtypeStruct(s, BF16) for s in dil_shapes] \
        + [jax.ShapeDtypeStruct((T, D), BF16)] * 2
    n_relayout = 3 * sum(d > 1 for _, d in DIL_CONFIGS)
    return pl.pallas_call(
        _in_proj_kernel,
        grid=(T // tm,),
        in_specs=[row(D), _const_spec((1, D)), _const_spec(w_na.shape), _const_spec(w_dil.shape),
                  _const_spec(w_gate.shape), _const_spec(nna.shape), _const_spec(ndil.shape), rot, rot,
                  _const_spec(bd.shape)],
        out_specs=[row(wna)] * 3 + dil_specs + [row(D)] * 2,
        out_shape=outs,
        scratch_shapes=[_relayout_scratch(tm)] * n_relayout,
        compiler_params=_cparams(1),
        name="in_proj",
    )(x2d, gain, w_na, w_dil, w_gate, nna, ndil, rot_a, rot_b, bd)


def _na_row_start(r, rows):
    return np.clip(r - NA_WIN_ROWS // 2, 0, rows - NA_WIN_ROWS)


NA_GROUP_ROWS = 4
NA_GROUP_WINDOW = 12


def _na_group_start(g, rows, xp=jnp):
    return xp.clip(g * NA_GROUP_ROWS - NA_WIN_ROWS // 2, 0, rows - NA_GROUP_WINDOW)


def _head_pair_rows(x, first):
    zero = jnp.zeros_like(x)
    return jnp.concatenate([jnp.where(first, x, zero), jnp.where(first, zero, x)], axis=0)


def _na_kernel(q_ref, k_ref, v_ref, b_ref, o_ref, *, rows):
    nq = NA_GROUP_ROWS * GRID_W
    nk = NA_GROUP_WINDOW * GRID_W
    pair = 2 * HEAD_DIM
    first = lax.broadcasted_iota(jnp.int32, (nq, pair), 1) < HEAD_DIM
    off = pl.multiple_of(_na_group_start(pl.program_id(1), rows) * GRID_W, GRID_W)
    kw = k_ref[pl.ds(off, nk), :]
    vw = v_ref[pl.ds(off, nk), :]
    q = q_ref[...]
    outs = []
    for hp in range(NA_HEADS // 2):
        sl = slice(hp * pair, (hp + 1) * pair)
        s = _dot_nt(_head_pair_rows(q[:, sl], first), kw[:, sl]) + b_ref[0, hp]
        m = jnp.max(s, axis=-1, keepdims=True)
        p = jnp.exp(s - m)
        l = jnp.sum(p, axis=-1, keepdims=True)
        o = _dot(p.astype(BF16), vw[:, sl]) / l
        outs.append(jnp.where(first, o[:nq], o[nq:]))
    o_ref[...] = jnp.concatenate(outs, axis=1).astype(BF16)


def _na_bias_table(rpb, rows):
    qc = np.arange(GRID_W)[:, None]
    kc = np.arange(GRID_W)[None, :]
    cs = np.clip(qc - NA_WIN_COLS // 2, 0, GRID_W - NA_WIN_COLS)
    col_ok = (kc >= cs) & (kc < cs + NA_WIN_COLS)
    dc = np.clip(kc - qc + NA_WIN_COLS - 1, 0, 2 * NA_WIN_COLS - 2)
    colb = rpb.astype(F32)[:, :, dc]
    H = rpb.shape[0]
    ngroups = rows // NA_GROUP_ROWS
    assert ngroups >= 3 and rows >= NA_GROUP_WINDOW
    tabs = []
    for g in (0, 1, ngroups - 1):
        key_row = _na_group_start(g, rows, np) + np.arange(NA_GROUP_WINDOW)
        per_row = []
        for r in range(g * NA_GROUP_ROWS, (g + 1) * NA_GROUP_ROWS):
            rs = _na_row_start(r, rows)
            row_ok = (key_row >= rs) & (key_row < rs + NA_WIN_ROWS)
            dr = np.clip(key_row - r + NA_WIN_ROWS - 1, 0, 2 * NA_WIN_ROWS - 2)
            ok = row_ok[:, None, None] & col_ok[None]
            b = jnp.where(ok[None], colb[:, dr], NEG_BIG)
            per_row.append(b.transpose(0, 2, 1, 3).reshape(H, GRID_W, NA_GROUP_WINDOW * GRID_W))
        t = jnp.stack(per_row, axis=1)
        tabs.append(t.reshape(H // 2, 2 * NA_GROUP_ROWS * GRID_W, NA_GROUP_WINDOW * GRID_W))
    return jnp.stack(tabs)


def _na_attn(qa, ka, va, bias_tab, batch, seq):
    T, W = qa.shape
    rows = seq // GRID_W
    ngroups = rows // NA_GROUP_ROWS

    def bias_idx(b, g):
        return (jnp.where(g == 0, 0, jnp.where(g == ngroups - 1, 2, 1)), 0, 0, 0)

    kv = pl.BlockSpec((seq, W), lambda b, g: (b, 0))
    qo = pl.BlockSpec((NA_GROUP_ROWS * GRID_W, W), lambda b, g: (b * ngroups + g, 0))
    return pl.pallas_call(
        functools.partial(_na_kernel, rows=rows),
        grid=(batch, ngroups),
        in_specs=[qo, kv, kv, pl.BlockSpec((1,) + bias_tab.shape[1:], bias_idx)],
        out_specs=qo,
        out_shape=jax.ShapeDtypeStruct((T, W), BF16),
        compiler_params=_cparams(2),
        name="na_attn",
    )(qa, ka, va, bias_tab)


DIL_QUERY_BLOCK = 128
DIL_BLOCKS_PER_ITER = 8


def _dil_kernel(q_ref, k_ref, v_ref, o_ref, lse_ref, *, length, side, qb, tiles):
    kb = qb + 2 * side
    nblk = length // qb
    qi = lax.broadcasted_iota(jnp.int32, (2 * qb, kb), 0) % qb
    kj = lax.broadcasted_iota(jnp.int32, (2 * qb, kb), 1)
    rel = kj - qi
    first = lax.broadcasted_iota(jnp.int32, (qb, LANES), 1) < HEAD_DIM

    def block(tile, i):
        lanes = slice(tile * LANES, (tile + 1) * LANES)
        qs = pl.multiple_of(i * qb, qb)
        ws = pl.multiple_of(jnp.clip(qs - side, 0, length - kb), side)
        delta = rel + (ws - qs)
        band = (delta >= -side) & (delta <= side)
        q = q_ref[0, pl.ds(qs, qb), lanes]
        k = k_ref[0, pl.ds(ws, kb), lanes]
        v = v_ref[0, pl.ds(ws, kb), lanes]
        s = jnp.where(band, _dot_nt(_head_pair_rows(q, first), k), NEG_BIG)
        m = jnp.max(s, axis=-1, keepdims=True)
        p = jnp.exp(s - m)
        l = jnp.sum(p, axis=-1, keepdims=True)
        o = _dot(p.astype(BF16), v) / l
        lse = jnp.broadcast_to(m + jnp.log(l), (2 * qb, LANES))
        o_ref[0, pl.ds(qs, qb), lanes] = jnp.where(first, o[:qb], o[qb:])
        lse_ref[0, pl.ds(qs, qb), lanes] = jnp.where(first, lse[:qb], lse[qb:])

    per_iter = min(DIL_BLOCKS_PER_ITER, nblk)

    def several(it, _):
        for tile in range(tiles):
            for u in range(per_iter):
                block(tile, it * per_iter + u)
        return 0

    lax.fori_loop(0, nblk // per_iter, several, 0)


def _dil_attn(q, k, v, batch, seq, window, dilation):
    length = seq // dilation
    side = window // (2 * dilation)
    qb = min(DIL_QUERY_BLOCK, length - 2 * side)
    width = dilation * DIL_GROUP_WIDTH
    shp = (batch, length, width)
    tiles = max(1, DIL_BLOCKS_PER_ITER // (length // qb))
    spec = pl.BlockSpec((1, length, tiles * LANES), lambda b, j: (b, 0, j))
    o, lse = pl.pallas_call(
        functools.partial(_dil_kernel, length=length, side=side, qb=qb, tiles=tiles),
        grid=(batch, width // (tiles * LANES)),
        in_specs=[spec, spec, spec],
        out_specs=[spec, spec],
        out_shape=[jax.ShapeDtypeStruct(shp, F32)] * 2,
        compiler_params=_cparams(2),
        name=f"dil_attn_d{dilation}",
    )(q.reshape(shp), k.reshape(shp), v.reshape(shp))
    return o.reshape(q.shape), lse.reshape(q.shape)


def _merge_kernel(x_ref, ona_ref, o1_ref, o2_ref, o3_ref, l1_ref, l2_ref, l3_ref, sgn_ref, sgd_ref,
                  wna_ref, wdil_ref, wout_ref, g_ref, x1_ref, h2_ref, *relayout_scratch):
    scratch = iter(relayout_scratch)

    def token_order(ref, dilation):
        return ref[...] if dilation == 1 else _from_residue_layout(ref[...], next(scratch), dilation)

    o1, o2, o3 = (token_order(r, d) for r, (_, d) in zip((o1_ref, o2_ref, o3_ref), DIL_CONFIGS))
    l1, l2, l3 = (token_order(r, d) for r, (_, d) in zip((l1_ref, l2_ref, l3_ref), DIL_CONFIGS))
    m = jnp.maximum(jnp.maximum(l1, l2), l3)
    w1, w2, w3 = jnp.exp(l1 - m), jnp.exp(l2 - m), jnp.exp(l3 - m)
    od = (w1 * o1 + w2 * o2 + w3 * o3) / (w1 + w2 + w3)
    merged = (sgn_ref[...].astype(F32) * _dot(ona_ref[...], wna_ref[...])
              + sgd_ref[...].astype(F32) * _dot(od.astype(BF16), wdil_ref[...]))
    x1 = x_ref[...] + _dot(merged.astype(BF16), wout_ref[...])
    x1_ref[...] = x1
    h2_ref[...] = _rms(x1, g_ref[...])


def _merge(x2d, ona, os_, ls_, sgn, sgd, wna, wdil, wout, gain, tm=512):
    T, D = x2d.shape
    row = lambda w: pl.BlockSpec((tm, w), lambda i: (i, 0))
    residue = [pl.BlockSpec((tm // d, d * DIL_GROUP_WIDTH), lambda i: (i, 0)) for _, d in DIL_CONFIGS]
    n_relayout = 2 * sum(d > 1 for _, d in DIL_CONFIGS)
    return pl.pallas_call(
        _merge_kernel,
        grid=(T // tm,),
        in_specs=[row(D), row(ona.shape[1])] + residue * 2 + [row(D), row(D),
                  _const_spec(wna.shape), _const_spec(wdil.shape), _const_spec(wout.shape), _const_spec((1, D))],
        out_specs=[row(D), row(D)],
        out_shape=[jax.ShapeDtypeStruct((T, D), F32)] * 2,
        scratch_shapes=[_relayout_scratch(tm)] * n_relayout,
        compiler_params=_cparams(1),
        name="merge",
    )(x2d, ona, *os_, *ls_, sgn, sgd, wna, wdil, wout, gain)


def _topk_rows(s, k, payload=None):
    n, tm = s.shape
    groups = n // SUBLANES
    sub = lax.broadcasted_iota(jnp.int32, (SUBLANES, tm), 0)
    rows = [sub + g * SUBLANES for g in range(groups)]
    parts = [s[g * SUBLANES:(g + 1) * SUBLANES] for g in range(groups)]
    tags = None if payload is None else [payload[g * SUBLANES:(g + 1) * SUBLANES] for g in range(groups)]
    vals, outs = [], []
    for _ in range(k):
        level = [(parts[g], rows[g], None if tags is None else tags[g]) for g in range(groups)]
        while len(level) > 1:
            merged = []
            for a, b in zip(level[0::2], level[1::2]):
                keep_a = a[0] >= b[0]
                merged.append((jnp.maximum(a[0], b[0]), jnp.where(keep_a, a[1], b[1]),
                               None if tags is None else jnp.where(keep_a, a[2], b[2])))
            if len(level) % 2:
                merged.append(level[-1])
            level = merged
        v8, r8, t8 = level[0]
        m = jnp.max(v8, axis=0, keepdims=True)
        idx = jnp.min(jnp.where(v8 == m, r8, n), axis=0, keepdims=True)
        vals.append(m)
        outs.append(idx if tags is None else jnp.max(jnp.where(r8 == idx, t8, -1), axis=0, keepdims=True))
        parts = [jnp.where(rows[g] == idx, -jnp.inf, parts[g]) for g in range(groups)]
    return jnp.concatenate(vals, axis=0), jnp.concatenate(outs, axis=0)


def _exact_bf16_pieces(x):
    p0 = x.astype(BF16)
    r1 = x - p0.astype(F32)
    p1 = r1.astype(BF16)
    p2 = (r1 - p1.astype(F32)).astype(BF16)
    return p0, p1, p2


def _peer_topk_kernel(h_ref, wq_ref, k1_ref, k2_ref, spread_ref, idx_ref, gates_ref, e_scr, g_scr):
    q = _dot(h_ref[...].astype(BF16), wq_ref[...])
    half = PEER_QDIM // 2
    s1 = _dot_nt(k1_ref[...], q[:, :half].astype(BF16))
    s2 = _dot_nt(k2_ref[...], q[:, half:].astype(BF16))
    v1, i1 = _topk_rows(s1, PEER_TOPK)
    v2, i2 = _topk_rows(s2, PEER_TOPK)
    keep = [PEER_TOPK // (i + 1) for i in range(PEER_TOPK)]
    pad = -sum(keep) % SUBLANES
    tm = v1.shape[1]
    cand = jnp.concatenate([v1[i:i + 1] + v2[:n] for i, n in enumerate(keep)]
                           + [jnp.full((pad, tm), -jnp.inf, F32)], axis=0)
    cidx = jnp.concatenate([i1[i:i + 1] * PEER_NKEYS + i2[:n] for i, n in enumerate(keep)]
                           + [jnp.full((pad, tm), -1, jnp.int32)], axis=0)
    sc, e = _topk_rows(cand, PEER_TOPK, payload=cidx)
    p = jnp.exp(sc - sc[0:1])
    head = pl.program_id(1)
    rows = pl.ds(pl.multiple_of(head * PEER_TOPK, PEER_TOPK), PEER_TOPK)
    e_scr[rows, :] = e
    g_scr[rows, :] = p / jnp.sum(p, axis=0, keepdims=True)

    @pl.when(head == PEER_HEADS - 1)
    def _():
        idx_ref[...] = e_scr[...].T * WORD_ROWS
        gates_ref[...] = sum(_dot(piece, spread_ref[...]) for piece in _exact_bf16_pieces(g_scr[...].T))


def _peer_topk(h2, wq, k1, k2, tm=1024):
    T, D = h2.shape
    spread = np.zeros((PEER_SLOTS, 2 * PEER_SLOTS), np.float32)
    spread[np.arange(PEER_SLOTS), 2 * np.arange(PEER_SLOTS)] = 1.0
    spread = jnp.asarray(spread, BF16)
    return pl.pallas_call(
        _peer_topk_kernel,
        grid=(T // tm, PEER_HEADS),
        in_specs=[pl.BlockSpec((tm, D), lambda i, h: (i, 0)), pl.BlockSpec((D, PEER_QDIM), lambda i, h: (0, h)),
                  _const_spec(k1.shape), _const_spec(k2.shape), _const_spec(spread.shape)],
        out_specs=[pl.BlockSpec((tm, PEER_SLOTS), lambda i, h: (i, 0)),
                   pl.BlockSpec((tm, 2 * PEER_SLOTS), lambda i, h: (i, 0))],
        out_shape=[jax.ShapeDtypeStruct((T, PEER_SLOTS), jnp.int32), jax.ShapeDtypeStruct((T, 2 * PEER_SLOTS), F32)],
        scratch_shapes=[pltpu.VMEM((PEER_SLOTS, tm), jnp.int32), pltpu.VMEM((PEER_SLOTS, tm), F32)],
        compiler_params=_cparams(2),
        name="peer_topk",
    )(h2, wq, k1, k2, spread)


WORD_ROWS = 4
GATHER_PITCH = 136
HALF = 512


def _pack_table(w):
    e, d = w.shape
    wb = w.astype(BF16)
    pairs = jnp.stack([wb[:, :HALF], wb[:, HALF:]], axis=-1)
    return lax.bitcast_convert_type(pairs, jnp.uint32).reshape(e * WORD_ROWS, LANES)


INDEX_LOOKAHEAD = 8


def _gather_rows(idx_ref, t, tab_ref, gbuf):
    starts = [idx_ref[t, j] for j in range(INDEX_LOOKAHEAD)]
    for j in range(PEER_SLOTS):
        if j + INDEX_LOOKAHEAD < PEER_SLOTS:
            starts.append(idx_ref[t, j + INDEX_LOOKAHEAD])
        row0 = pl.multiple_of(starts[j], WORD_ROWS)
        gbuf[pl.ds(j, WORD_ROWS, stride=GATHER_PITCH), :] = tab_ref[pl.ds(row0, WORD_ROWS), :]


class _StagedIndices:
    def __init__(self, idx_hbm, bufs, sems):
        self.hbm, self.bufs, self.sems = idx_hbm, bufs, sems
        self.half = bufs[0].shape[0]
        self.step = pl.program_id(0)
        self.last = pl.num_programs(0) - 1

        @pl.when(self.step == 0)
        def _():
            for k in range(2):
                self._copy(self.step, k).start()

    def _copy(self, step, k):
        rows = pl.ds((2 * step + k) * self.half, self.half)
        return pltpu.make_async_copy(self.hbm.at[rows], self.bufs[k], self.sems.at[k])

    def gather(self, t, tab_ref, gbuf):
        k, local = divmod(t, self.half)
        if local == 0:
            self._copy(self.step, k).wait()
        _gather_rows(self.bufs[k], local, tab_ref, gbuf)
        if local == self.half - 1:
            self._copy(jnp.minimum(self.step + 1, self.last), k).start()

    def finish(self):
        @pl.when(self.step == self.last)
        def _():
            for k in range(2):
                self._copy(self.last, k).wait()


def _gathered_matrix(gbuf):
    planes = [pltpu.bitcast(gbuf[i * GATHER_PITCH:i * GATHER_PITCH + PEER_SLOTS, :], BF16) for i in range(WORD_ROWS)]
    return jnp.concatenate(planes, axis=1)


def _lhs16(rows, width):
    r = lax.broadcasted_iota(jnp.int32, (16, width), 0)
    out = jnp.zeros((16, width), F32)
    for k, v in enumerate(rows):
        out = jnp.where(r == k, v, out)
    return out.astype(BF16)


PEER_TOKENS_PER_STEP = 64
PIPELINE_LAG = 2


def _pipelined_tokens(tb, gather, compute, bufs):
    n = len(bufs)
    for t in range(tb + PIPELINE_LAG):
        if t < tb:
            gather(t, bufs[t % n])
        if t >= PIPELINE_LAG:
            compute(t - PIPELINE_LAG, bufs[(t - PIPELINE_LAG) % n])


def _peer_u_kernel(idx_hbm, h_ref, tab_ref, g_ref, act_ref, hhi_ref, hlo_ref, c_ref, ga0, ga1, gb0, gb1,
                   idx_a, idx_b, idx_sems, *, tb):
    h = h_ref[...]
    hi = h.astype(BF16).astype(F32)
    hhi_ref[...] = hi
    hlo_ref[...] = h - hi
    even = (lax.broadcasted_iota(jnp.int32, (SUBLANES, 2 * PEER_SLOTS), 1) % 2) == 0
    indices = _StagedIndices(idx_hbm, (idx_a, idx_b), idx_sems)

    def compute(t, gbuf):
        row = pl.ds(t, 1)
        lhs = _lhs16([hhi_ref[row, :HALF], hhi_ref[row, HALF:], hlo_ref[row, :HALF], hlo_ref[row, HALF:]], HALF)
        res = _dot_nt(lhs, _gathered_matrix(gbuf))[:SUBLANES]
        s = res + pltpu.roll(res, 6, 0)
        c_ref[row, :] = jnp.where(even, s, pltpu.roll(s, 7, 0))[0:1]

    _pipelined_tokens(tb, lambda t, g: indices.gather(t, tab_ref, g), compute, (ga0, ga1, gb0, gb1))

    c = c_ref[...]
    a = c + pltpu.roll(c, 2 * PEER_SLOTS - 1, 1)
    gelu = 0.5 * a * (1.0 + lax.erf(a * np.float32(np.sqrt(0.5))))
    lane_even = (lax.broadcasted_iota(jnp.int32, a.shape, 1) % 2) == 0
    act_ref[...] = jnp.where(lane_even, gelu * g_ref[...], 0.0)
    indices.finish()


def _gather_scratch(tb):
    return ([pltpu.VMEM((WORD_ROWS * GATHER_PITCH, LANES), jnp.uint32)] * 4
            + [pltpu.SMEM((tb // 2, PEER_SLOTS), jnp.int32)] * 2 + [pltpu.SemaphoreType.DMA((2,))])


def _peer_u(idx, h2, tab, gates, tb=PEER_TOKENS_PER_STEP):
    T, D = h2.shape
    wide = pl.BlockSpec((tb, 2 * PEER_SLOTS), lambda i: (i, 0))
    return pl.pallas_call(
        functools.partial(_peer_u_kernel, tb=tb),
        grid=(T // tb,),
        in_specs=[pl.BlockSpec(memory_space=pl.ANY),
                  pl.BlockSpec((tb, D), lambda i: (i, 0)), _const_spec(tab.shape), wide],
        out_specs=wide,
        out_shape=jax.ShapeDtypeStruct((T, 2 * PEER_SLOTS), F32),
        scratch_shapes=[pltpu.VMEM((tb, D), F32)] * 2 + [pltpu.VMEM((tb, 2 * PEER_SLOTS), F32)] + _gather_scratch(tb),
        compiler_params=_cparams(1),
        name="peer_u",
    )(idx, h2, tab, gates)


def _peer_v_kernel(idx_hbm, act_ref, x_ref, tab_ref, o_ref, ahi_ref, alo_ref, bhi_ref, blo_ref, acc_ref,
                   ga0, ga1, gb0, gb1, idx_a, idx_b, idx_sems, *, tb):
    a = act_ref[...]
    b = pltpu.roll(a, 1, 1)
    for src, hi_ref, lo_ref in ((a, ahi_ref, alo_ref), (b, bhi_ref, blo_ref)):
        hi = src.astype(BF16).astype(F32)
        hi_ref[...] = hi
        lo_ref[...] = src - hi
    indices = _StagedIndices(idx_hbm, (idx_a, idx_b), idx_sems)

    def compute(t, gbuf):
        row = pl.ds(t, 1)
        lhs = _lhs16([ahi_ref[row, :], bhi_ref[row, :], alo_ref[row, :], blo_ref[row, :]], 2 * PEER_SLOTS)
        res = _dot(lhs, _gathered_matrix(gbuf))[:SUBLANES]
        s = res + pltpu.roll(res, 6, 0)
        acc_ref[row, :HALF] = s[0:1]
        acc_ref[row, HALF:] = s[1:2]

    _pipelined_tokens(tb, lambda t, g: indices.gather(t, tab_ref, g), compute, (ga0, ga1, gb0, gb1))
    o_ref[...] = x_ref[...] + acc_ref[...]
    indices.finish()


def _peer_v(idx, act, x2d, tab, tb=PEER_TOKENS_PER_STEP):
    T, D = x2d.shape
    tok = pl.BlockSpec((tb, D), lambda i: (i, 0))
    wide = pl.BlockSpec((tb, 2 * PEER_SLOTS), lambda i: (i, 0))
    return pl.pallas_call(
        functools.partial(_peer_v_kernel, tb=tb),
        grid=(T // tb,),
        in_specs=[pl.BlockSpec(memory_space=pl.ANY), wide, tok, _const_spec(tab.shape)],
        out_specs=tok,
        out_shape=jax.ShapeDtypeStruct((T, D), F32),
        scratch_shapes=[pltpu.VMEM((tb, 2 * PEER_SLOTS), F32)] * 4 + [pltpu.VMEM((tb, D), F32)] + _gather_scratch(tb),
        compiler_params=_cparams(1),
        name="peer_v",
    )(idx, act, x2d, tab)


def _ple_kernel(x_ref, p_ref, g_ref, wg_ref, wp_ref, o_ref):
    x = x_ref[...]
    h = _rms(x, g_ref[...]).astype(BF16)
    o_ref[...] = x + jax.nn.sigmoid(_dot(h, wg_ref[...])) * _dot(p_ref[...].astype(BF16), wp_ref[...])


def _ple(x2d, p2d, gain, wg, wp, tm=512):
    T, D = x2d.shape
    row = lambda w: pl.BlockSpec((tm, w), lambda i: (i, 0))
    return pl.pallas_call(
        _ple_kernel,
        grid=(T // tm,),
        in_specs=[row(D), row(p2d.shape[1]), _const_spec((1, D)), _const_spec(wg.shape), _const_spec(wp.shape)],
        out_specs=row(D),
        out_shape=jax.ShapeDtypeStruct((T, D), F32),
        compiler_params=_cparams(1),
        name="ple",
    )(x2d, p2d, gain, wg, wp)


def _rotary_tables(seq):
    half = ROT_DIM // 2
    inv_freq = jnp.power(jnp.float32(ROPE_THETA), -jnp.arange(half, dtype=F32) * 2.0 / ROT_DIM)
    ang = jnp.arange(seq).astype(F32)[:, None] * inv_freq[None, :]
    cos, sin = jnp.cos(ang), jnp.sin(ang)
    pad = HEAD_DIM - ROT_DIM
    ra = jnp.concatenate([cos, cos, jnp.ones((seq, pad), F32)], axis=1)
    rb = jnp.concatenate([-sin, sin, jnp.zeros((seq, pad), F32)], axis=1)
    return jnp.tile(ra, (1, DIL_HEADS_PER_GROUP)), jnp.tile(rb, (1, DIL_HEADS_PER_GROUP))


def kernel(x, p, norm_mix, w_in, qk_norm_na, na_rel_bias, qk_norm_dil, w_branch_na, w_branch_dil, w_out, norm_ffn,
           peer_w_query, peer_sub_keys, peer_expert_u, peer_expert_v, norm_ple, w_ple_gate, w_ple):
    B, S, D = x.shape
    T = B * S
    depth = w_in.shape[0]
    wna = NA_HEADS * HEAD_DIM
    wdil = len(DIL_CONFIGS) * DIL_GROUP_WIDTH
    rot_a, rot_b = _rotary_tables(S)
    x2d = x.reshape(T, D)
    for i in range(depth):
        wi = w_in[i].astype(BF16)
        nna = jnp.tile(qk_norm_na[i], (1, NA_HEADS))
        ndil = jnp.tile(qk_norm_dil[i], (1, DIL_HEADS_PER_GROUP))
        (qa, ka, va, q1, k1, v1, q2, k2, v2, q3, k3, v3, sgn, sgd) = _in_proj(
            x2d, norm_mix[i][None], wi[:, :3 * wna], wi[:, 3 * wna:3 * wna + 3 * wdil], wi[:, 3 * wna + 3 * wdil:],
            nna, ndil, rot_a, rot_b, S)

        ona = _na_attn(qa, ka, va, _na_bias_table(na_rel_bias[i], S // GRID_W), B, S)
        dil = [_dil_attn(q, k, v, B, S, window, dilation)
               for (q, k, v), (window, dilation) in zip(((q1, k1, v1), (q2, k2, v2), (q3, k3, v3)), DIL_CONFIGS)]

        x1, h2 = _merge(x2d, ona, [o for o, _ in dil], [l for _, l in dil], sgn, sgd,
                        w_branch_na[i].astype(BF16), w_branch_dil[i].astype(BF16), w_out[i].astype(BF16),
                        norm_ffn[i][None])

        idx, gates = _peer_topk(h2, peer_w_query[i].astype(BF16), peer_sub_keys[i, 0].astype(BF16),
                                peer_sub_keys[i, 1].astype(BF16))
        act = _peer_u(idx, h2, _pack_table(peer_expert_u[i]), gates)
        x2 = _peer_v(idx, act, x1, _pack_table(peer_expert_v[i]))

        x2d = _ple(x2, p[i].reshape(T, -1), norm_ple[i][None], w_ple_gate[i].astype(BF16), w_ple[i].astype(BF16))
    return x2d.reshape(B, S, D)
```

```python
import functools

import numpy as np
import jax
import jax.numpy as jnp
from jax import lax
from jax.experimental import pallas as pl
from jax.experimental.pallas import tpu as pltpu

F32 = jnp.float32
BF16 = jnp.bfloat16

HEAD_DIM = 64
GRID_W = 64
NA_HEADS = 8
NA_WIN_ROWS = 8
NA_WIN_COLS = 16
DIL_CONFIGS = ((128, 1), (512, 4), (2048, 16))
DIL_HEADS_PER_GROUP = 4
DIL_GROUP_WIDTH = DIL_HEADS_PER_GROUP * HEAD_DIM
ROT_DIM = HEAD_DIM // 4
ROPE_THETA = 500000.0
PEER_HEADS = 8
PEER_NKEYS = 128
PEER_QDIM = 256
PEER_TOPK = 16
PEER_SLOTS = PEER_HEADS * PEER_TOPK
RMS_EPS = 1e-6
NEG_BIG = -1e30

LANES = 128
SUBLANES = 8
VMEM_LIMIT_BYTES = 48 * 1024 * 1024


def _cparams(n_axes):
    return pltpu.CompilerParams(dimension_semantics=("arbitrary",) * n_axes, vmem_limit_bytes=VMEM_LIMIT_BYTES)


def _const_spec(shape):
    nd = len(shape)
    return pl.BlockSpec(shape, lambda *_: (0,) * nd, pipeline_mode=pl.Buffered(1))


def _rms(x, gain):
    return x * lax.rsqrt(jnp.mean(x * x, axis=-1, keepdims=True) + RMS_EPS) * gain


def _dot(a, b):
    return jnp.dot(a, b, preferred_element_type=F32)


def _dot_nt(a, b):
    return lax.dot_general(a, b, (((1,), (1,)), ((), ())), preferred_element_type=F32)


def _relayout_scratch(tm):
    return pltpu.VMEM((DIL_GROUP_WIDTH // LANES, tm, LANES), F32)


def _to_residue_layout(x, scr_ref, d):
    tm = x.shape[0]
    slabs = DIL_GROUP_WIDTH // LANES
    for c in range(slabs):
        scr_ref[c] = x[:, c * LANES:(c + 1) * LANES]
    return jnp.concatenate([scr_ref[c, pl.ds(r, tm // d, stride=d), :] for r in range(d) for c in range(slabs)], axis=1)


def _from_residue_layout(y, scr_ref, d):
    rows = y.shape[0]
    slabs = DIL_GROUP_WIDTH // LANES
    for r in range(d):
        for c in range(slabs):
            lo = r * DIL_GROUP_WIDTH + c * LANES
            scr_ref[c, pl.ds(r, rows, stride=d), :] = y[:, lo:lo + LANES]
    return jnp.concatenate([scr_ref[c] for c in range(slabs)], axis=1)


def _head_rms(q, bd_ref, gain):
    outs = []
    for c in range(q.shape[1] // 256):
        qc = q[:, c * 256:(c + 1) * 256]
        sq = qc * qc
        hi = sq.astype(BF16)
        lo = (sq - hi.astype(F32)).astype(BF16)
        ssq = _dot(hi, bd_ref[...]) + _dot(lo, bd_ref[...])
        outs.append(qc * lax.rsqrt(ssq * (1.0 / HEAD_DIM) + RMS_EPS))
    return jnp.concatenate(outs, axis=1) * gain


def _rotary(q, ra, rb):
    lane = lax.broadcasted_iota(jnp.int32, q.shape, 1) % HEAD_DIM
    partner = jnp.where(lane < ROT_DIM // 2, pltpu.roll(q, 256 - ROT_DIM // 2, 1), pltpu.roll(q, ROT_DIM // 2, 1))
    return q * ra + partner * rb


def _in_proj_kernel(x_ref, g_ref, wna_ref, wdil_ref, wgate_ref, nna_ref, ndil_ref, ra_ref, rb_ref, bd_ref,
                    qa_ref, ka_ref, va_ref,
                    q1_ref, k1_ref, v1_ref, q2_ref, k2_ref, v2_ref, q3_ref, k3_ref, v3_ref,
                    sgn_ref, sgd_ref, *relayout_scratch):
    scratch = iter(relayout_scratch)
    h = _rms(x_ref[...], g_ref[...]).astype(BF16)
    scale = HEAD_DIM ** -0.5

    na = _dot(h, wna_ref[...])
    wna = NA_HEADS * HEAD_DIM
    qa_ref[...] = (_head_rms(na[:, :wna], bd_ref, nna_ref[0:1, :]) * scale).astype(BF16)
    ka_ref[...] = _head_rms(na[:, wna:2 * wna], bd_ref, nna_ref[1:2, :]).astype(BF16)
    va_ref[...] = na[:, 2 * wna:].astype(BF16)

    dil = _dot(h, wdil_ref[...])
    wd = len(DIL_CONFIGS) * DIL_GROUP_WIDTH
    ra = ra_ref[...]
    rb = rb_ref[...]
    q_refs = (q1_ref, q2_ref, q3_ref)
    k_refs = (k1_ref, k2_ref, k3_ref)
    v_refs = (v1_ref, v2_ref, v3_ref)
    for g in range(len(DIL_CONFIGS)):
        lo = g * DIL_GROUP_WIDTH
        hi = lo + DIL_GROUP_WIDTH
        q = _head_rms(dil[:, lo:hi], bd_ref, ndil_ref[0:1, :])
        k = _head_rms(dil[:, wd + lo:wd + hi], bd_ref, ndil_ref[1:2, :])
        dilation = DIL_CONFIGS[g][1]
        for ref, val in ((q_refs[g], _rotary(q, ra, rb) * scale), (k_refs[g], _rotary(k, ra, rb)),
                         (v_refs[g], dil[:, 2 * wd + lo:2 * wd + hi])):
            if dilation > 1:
                val = _to_residue_layout(val, next(scratch), dilation)
            ref[...] = val.astype(BF16)

    gate = _dot(h, wgate_ref[...])
    d = sgn_ref.shape[1]
    sgn_ref[...] = jax.nn.sigmoid(gate[:, :d]).astype(BF16)
    sgd_ref[...] = jax.nn.sigmoid(gate[:, d:]).astype(BF16)


def _in_proj(x2d, gain, w_na, w_dil, w_gate, nna, ndil, rot_a, rot_b, seq, tm=512):
    T, D = x2d.shape
    bd = jnp.asarray(np.kron(np.eye(256 // HEAD_DIM), np.ones((HEAD_DIM, HEAD_DIM))), BF16)
    wna = NA_HEADS * HEAD_DIM
    nseq = seq // tm
    row = lambda w: pl.BlockSpec((tm, w), lambda i: (i, 0))
    rot = pl.BlockSpec((tm, DIL_GROUP_WIDTH), lambda i: (i % nseq, 0))
    dil_shapes = [(T // d, d * DIL_GROUP_WIDTH) for _, d in DIL_CONFIGS for _ in range(3)]
    dil_specs = [pl.BlockSpec((tm // d, d * DIL_GROUP_WIDTH), lambda i: (i, 0)) for _, d in DIL_CONFIGS for _ in range(3)]
    outs = [jax.ShapeDtypeStruct((T, wna), BF16)] * 3 + [jax.ShapeDtypeStruct(s, BF16) for s in dil_shapes] \
        + [jax.ShapeDtypeStruct((T, D), BF16)] * 2
    n_relayout = 3 * sum(d > 1 for _, d in DIL_CONFIGS)
    return pl.pallas_call(
        _in_proj_kernel,
        grid=(T // tm,),
        in_specs=[row(D), _const_spec((1, D)), _const_spec(w_na.shape), _const_spec(w_dil.shape),
                  _const_spec(w_gate.shape), _const_spec(nna.shape), _const_spec(ndil.shape), rot, rot,
                  _const_spec(bd.shape)],
        out_specs=[row(wna)] * 3 + dil_specs + [row(D)] * 2,
        out_shape=outs,
        scratch_shapes=[_relayout_scratch(tm)] * n_relayout,
        compiler_params=_cparams(1),
        name="in_proj",
    )(x2d, gain, w_na, w_dil, w_gate, nna, ndil, rot_a, rot_b, bd)


def _na_row_start(r, rows):
    return np.clip(r - NA_WIN_ROWS // 2, 0, rows - NA_WIN_ROWS)


NA_GROUP_ROWS = 4
NA_GROUP_WINDOW = 12


def _na_group_start(g, rows, xp=jnp):
    return xp.clip(g * NA_GROUP_ROWS - NA_WIN_ROWS // 2, 0, rows - NA_GROUP_WINDOW)


def _head_pair_rows(x, first):
    zero = jnp.zeros_like(x)
    return jnp.concatenate([jnp.where(first, x, zero), jnp.where(first, zero, x)], axis=0)


def _na_kernel(q_ref, k_ref, v_ref, b_ref, o_ref, *, rows):
    nq = NA_GROUP_ROWS * GRID_W
    nk = NA_GROUP_WINDOW * GRID_W
    pair = 2 * HEAD_DIM
    first = lax.broadcasted_iota(jnp.int32, (nq, pair), 1) < HEAD_DIM
    off = pl.multiple_of(_na_group_start(pl.program_id(1), rows) * GRID_W, GRID_W)
    kw = k_ref[pl.ds(off, nk), :]
    vw = v_ref[pl.ds(off, nk), :]
    q = q_ref[...]
    outs = []
    for hp in range(NA_HEADS // 2):
        sl = slice(hp * pair, (hp + 1) * pair)
        s = _dot_nt(_head_pair_rows(q[:, sl], first), kw[:, sl]) + b_ref[0, hp]
        m = jnp.max(s, axis=-1, keepdims=True)
        p = jnp.exp(s - m)
        l = jnp.sum(p, axis=-1, keepdims=True)
        o = _dot(p.astype(BF16), vw[:, sl]) / l
        outs.append(jnp.where(first, o[:nq], o[nq:]))
    o_ref[...] = jnp.concatenate(outs, axis=1).astype(BF16)


def _na_bias_table(rpb, rows):
    qc = np.arange(GRID_W)[:, None]
    kc = np.arange(GRID_W)[None, :]
    cs = np.clip(qc - NA_WIN_COLS // 2, 0, GRID_W - NA_WIN_COLS)
    col_ok = (kc >= cs) & (kc < cs + NA_WIN_COLS)
    dc = np.clip(kc - qc + NA_WIN_COLS - 1, 0, 2 * NA_WIN_COLS - 2)
    colb = rpb.astype(F32)[:, :, dc]
    H = rpb.shape[0]
    ngroups = rows // NA_GROUP_ROWS
    assert ngroups >= 3 and rows >= NA_GROUP_WINDOW
    tabs = []
    for g in (0, 1, ngroups - 1):
        key_row = _na_group_start(g, rows, np) + np.arange(NA_GROUP_WINDOW)
        per_row = []
        for r in range(g * NA_GROUP_ROWS, (g + 1) * NA_GROUP_ROWS):
            rs = _na_row_start(r, rows)
            row_ok = (key_row >= rs) & (key_row < rs + NA_WIN_ROWS)
            dr = np.clip(key_row - r + NA_WIN_ROWS - 1, 0, 2 * NA_WIN_ROWS - 2)
            ok = row_ok[:, None, None] & col_ok[None]
            b = jnp.where(ok[None], colb[:, dr], NEG_BIG)
            per_row.append(b.transpose(0, 2, 1, 3).reshape(H, GRID_W, NA_GROUP_WINDOW * GRID_W))
        t = jnp.stack(per_row, axis=1)
        tabs.append(t.reshape(H // 2, 2 * NA_GROUP_ROWS * GRID_W, NA_GROUP_WINDOW * GRID_W))
    return jnp.stack(tabs)


def _na_attn(qa, ka, va, bias_tab, batch, seq):
    T, W = qa.shape
    rows = seq // GRID_W
    ngroups = rows // NA_GROUP_ROWS

    def bias_idx(b, g):
        return (jnp.where(g == 0, 0, jnp.where(g == ngroups - 1, 2, 1)), 0, 0, 0)

    kv = pl.BlockSpec((seq, W), lambda b, g: (b, 0))
    qo = pl.BlockSpec((NA_GROUP_ROWS * GRID_W, W), lambda b, g: (b * ngroups + g, 0))
    return pl.pallas_call(
        functools.partial(_na_kernel, rows=rows),
        grid=(batch, ngroups),
        in_specs=[qo, kv, kv, pl.BlockSpec((1,) + bias_tab.shape[1:], bias_idx)],
        out_specs=qo,
        out_shape=jax.ShapeDtypeStruct((T, W), BF16),
        compiler_params=_cparams(2),
        name="na_attn",
    )(qa, ka, va, bias_tab)


DIL_QUERY_BLOCK = 128
DIL_BLOCKS_PER_ITER = 8


def _dil_kernel(q_ref, k_ref, v_ref, o_ref, lse_ref, *, length, side, qb, tiles):
    kb = qb + 2 * side
    nblk = length // qb
    qi = lax.broadcasted_iota(jnp.int32, (2 * qb, kb), 0) % qb
    kj = lax.broadcasted_iota(jnp.int32, (2 * qb, kb), 1)
    rel = kj - qi
    first = lax.broadcasted_iota(jnp.int32, (qb, LANES), 1) < HEAD_DIM

    def block(tile, i):
        lanes = slice(tile * LANES, (tile + 1) * LANES)
        qs = pl.multiple_of(i * qb, qb)
        ws = pl.multiple_of(jnp.clip(qs - side, 0, length - kb), side)
        delta = rel + (ws - qs)
        band = (delta >= -side) & (delta <= side)
        q = q_ref[0, pl.ds(qs, qb), lanes]
        k = k_ref[0, pl.ds(ws, kb), lanes]
        v = v_ref[0, pl.ds(ws, kb), lanes]
        s = jnp.where(band, _dot_nt(_head_pair_rows(q, first), k), NEG_BIG)
        m = jnp.max(s, axis=-1, keepdims=True)
        p = jnp.exp(s - m)
        l = jnp.sum(p, axis=-1, keepdims=True)
        o = _dot(p.astype(BF16), v) / l
        lse = jnp.broadcast_to(m + jnp.log(l), (2 * qb, LANES))
        o_ref[0, pl.ds(qs, qb), lanes] = jnp.where(first, o[:qb], o[qb:])
        lse_ref[0, pl.ds(qs, qb), lanes] = jnp.where(first, lse[:qb], lse[qb:])

    per_iter = min(DIL_BLOCKS_PER_ITER, nblk)

    def several(it, _):
        for tile in range(tiles):
            for u in range(per_iter):
                block(tile, it * per_iter + u)
        return 0

    lax.fori_loop(0, nblk // per_iter, several, 0)


def _dil_attn(q, k, v, batch, seq, window, dilation):
    length = seq // dilation
    side = window // (2 * dilation)
    qb = min(DIL_QUERY_BLOCK, length - 2 * side)
    width = dilation * DIL_GROUP_WIDTH
    shp = (batch, length, width)
    tiles = max(1, DIL_BLOCKS_PER_ITER // (length // qb))
    spec = pl.BlockSpec((1, length, tiles * LANES), lambda b, j: (b, 0, j))
    o, lse = pl.pallas_call(
        functools.partial(_dil_kernel, length=length, side=side, qb=qb, tiles=tiles),
        grid=(batch, width // (tiles * LANES)),
        in_specs=[spec, spec, spec],
        out_specs=[spec, spec],
        out_shape=[jax.ShapeDtypeStruct(shp, F32)] * 2,
        compiler_params=_cparams(2),
        name=f"dil_attn_d{dilation}",
    )(q.reshape(shp), k.reshape(shp), v.reshape(shp))
    return o.reshape(q.shape), lse.reshape(q.shape)


def _merge_kernel(x_ref, ona_ref, o1_ref, o2_ref, o3_ref, l1_ref, l2_ref, l3_ref, sgn_ref, sgd_ref,
                  wna_ref, wdil_ref, wout_ref, g_ref, x1_ref, h2_ref, *relayout_scratch):
    scratch = iter(relayout_scratch)

    def token_order(ref, dilation):
        return ref[...] if dilation == 1 else _from_residue_layout(ref[...], next(scratch), dilation)

    o1, o2, o3 = (token_order(r, d) for r, (_, d) in zip((o1_ref, o2_ref, o3_ref), DIL_CONFIGS))
    l1, l2, l3 = (token_order(r, d) for r, (_, d) in zip((l1_ref, l2_ref, l3_ref), DIL_CONFIGS))
    m = jnp.maximum(jnp.maximum(l1, l2), l3)
    w1, w2, w3 = jnp.exp(l1 - m), jnp.exp(l2 - m), jnp.exp(l3 - m)
    od = (w1 * o1 + w2 * o2 + w3 * o3) / (w1 + w2 + w3)
    merged = (sgn_ref[...].astype(F32) * _dot(ona_ref[...], wna_ref[...])
              + sgd_ref[...].astype(F32) * _dot(od.astype(BF16), wdil_ref[...]))
    x1 = x_ref[...] + _dot(merged.astype(BF16), wout_ref[...])
    x1_ref[...] = x1
    h2_ref[...] = _rms(x1, g_ref[...])


def _merge(x2d, ona, os_, ls_, sgn, sgd, wna, wdil, wout, gain, tm=512):
    T, D = x2d.shape
    row = lambda w: pl.BlockSpec((tm, w), lambda i: (i, 0))
    residue = [pl.BlockSpec((tm // d, d * DIL_GROUP_WIDTH), lambda i: (i, 0)) for _, d in DIL_CONFIGS]
    n_relayout = 2 * sum(d > 1 for _, d in DIL_CONFIGS)
    return pl.pallas_call(
        _merge_kernel,
        grid=(T // tm,),
        in_specs=[row(D), row(ona.shape[1])] + residue * 2 + [row(D), row(D),
                  _const_spec(wna.shape), _const_spec(wdil.shape), _const_spec(wout.shape), _const_spec((1, D))],
        out_specs=[row(D), row(D)],
        out_shape=[jax.ShapeDtypeStruct((T, D), F32)] * 2,
        scratch_shapes=[_relayout_scratch(tm)] * n_relayout,
        compiler_params=_cparams(1),
        name="merge",
    )(x2d, ona, *os_, *ls_, sgn, sgd, wna, wdil, wout, gain)


def _topk_rows(s, k, payload=None):
    n, tm = s.shape
    groups = n // SUBLANES
    sub = lax.broadcasted_iota(jnp.int32, (SUBLANES, tm), 0)
    rows = [sub + g * SUBLANES for g in range(groups)]
    parts = [s[g * SUBLANES:(g + 1) * SUBLANES] for g in range(groups)]
    tags = None if payload is None else [payload[g * SUBLANES:(g + 1) * SUBLANES] for g in range(groups)]
    vals, outs = [], []
    for _ in range(k):
        level = [(parts[g], rows[g], None if tags is None else tags[g]) for g in range(groups)]
        while len(level) > 1:
            merged = []
            for a, b in zip(level[0::2], level[1::2]):
                keep_a = a[0] >= b[0]
                merged.append((jnp.maximum(a[0], b[0]), jnp.where(keep_a, a[1], b[1]),
                               None if tags is None else jnp.where(keep_a, a[2], b[2])))
            if len(level) % 2:
                merged.append(level[-1])
            level = merged
        v8, r8, t8 = level[0]
        m = jnp.max(v8, axis=0, keepdims=True)
        idx = jnp.min(jnp.where(v8 == m, r8, n), axis=0, keepdims=True)
        vals.append(m)
        outs.append(idx if tags is None else jnp.max(jnp.where(r8 == idx, t8, -1), axis=0, keepdims=True))
        parts = [jnp.where(rows[g] == idx, -jnp.inf, parts[g]) for g in range(groups)]
    return jnp.concatenate(vals, axis=0), jnp.concatenate(outs, axis=0)


def _exact_bf16_pieces(x):
    p0 = x.astype(BF16)
    r1 = x - p0.astype(F32)
    p1 = r1.astype(BF16)
    p2 = (r1 - p1.astype(F32)).astype(BF16)
    return p0, p1, p2


def _peer_topk_kernel(h_ref, wq_ref, k1_ref, k2_ref, spread_ref, idx_ref, gates_ref, e_scr, g_scr):
    q = _dot(h_ref[...].astype(BF16), wq_ref[...])
    half = PEER_QDIM // 2
    s1 = _dot_nt(k1_ref[...], q[:, :half].astype(BF16))
    s2 = _dot_nt(k2_ref[...], q[:, half:].astype(BF16))
    v1, i1 = _topk_rows(s1, PEER_TOPK)
    v2, i2 = _topk_rows(s2, PEER_TOPK)
    keep = [PEER_TOPK // (i + 1) for i in range(PEER_TOPK)]
    pad = -sum(keep) % SUBLANES
    tm = v1.shape[1]
    cand = jnp.concatenate([v1[i:i + 1] + v2[:n] for i, n in enumerate(keep)]
                           + [jnp.full((pad, tm), -jnp.inf, F32)], axis=0)
    cidx = jnp.concatenate([i1[i:i + 1] * PEER_NKEYS + i2[:n] for i, n in enumerate(keep)]
                           + [jnp.full((pad, tm), -1, jnp.int32)], axis=0)
    sc, e = _topk_rows(cand, PEER_TOPK, payload=cidx)
    p = jnp.exp(sc - sc[0:1])
    head = pl.program_id(1)
    rows = pl.ds(pl.multiple_of(head * PEER_TOPK, PEER_TOPK), PEER_TOPK)
    e_scr[rows, :] = e
    g_scr[rows, :] = p / jnp.sum(p, axis=0, keepdims=True)

    @pl.when(head == PEER_HEADS - 1)
    def _():
        idx_ref[...] = e_scr[...].T * WORD_ROWS
        gates_ref[...] = sum(_dot(piece, spread_ref[...]) for piece in _exact_bf16_pieces(g_scr[...].T))


def _peer_topk(h2, wq, k1, k2, tm=1024):
    T, D = h2.shape
    spread = np.zeros((PEER_SLOTS, 2 * PEER_SLOTS), np.float32)
    spread[np.arange(PEER_SLOTS), 2 * np.arange(PEER_SLOTS)] = 1.0
    spread = jnp.asarray(spread, BF16)
    return pl.pallas_call(
        _peer_topk_kernel,
        grid=(T // tm, PEER_HEADS),
        in_specs=[pl.BlockSpec((tm, D), lambda i, h: (i, 0)), pl.BlockSpec((D, PEER_QDIM), lambda i, h: (0, h)),
                  _const_spec(k1.shape), _const_spec(k2.shape), _const_spec(spread.shape)],
        out_specs=[pl.BlockSpec((tm, PEER_SLOTS), lambda i, h: (i, 0)),
                   pl.BlockSpec((tm, 2 * PEER_SLOTS), lambda i, h: (i, 0))],
        out_shape=[jax.ShapeDtypeStruct((T, PEER_SLOTS), jnp.int32), jax.ShapeDtypeStruct((T, 2 * PEER_SLOTS), F32)],
        scratch_shapes=[pltpu.VMEM((PEER_SLOTS, tm), jnp.int32), pltpu.VMEM((PEER_SLOTS, tm), F32)],
        compiler_params=_cparams(2),
        name="peer_topk",
    )(h2, wq, k1, k2, spread)


WORD_ROWS = 4
GATHER_PITCH = 136
HALF = 512


def _pack_table(w):
    e, d = w.shape
    wb = w.astype(BF16)
    pairs = jnp.stack([wb[:, :HALF], wb[:, HALF:]], axis=-1)
    return lax.bitcast_convert_type(pairs, jnp.uint32).reshape(e * WORD_ROWS, LANES)


INDEX_LOOKAHEAD = 8


def _gather_rows(idx_ref, t, tab_ref, gbuf):
    starts = [idx_ref[t, j] for j in range(INDEX_LOOKAHEAD)]
    for j in range(PEER_SLOTS):
        if j + INDEX_LOOKAHEAD < PEER_SLOTS:
            starts.append(idx_ref[t, j + INDEX_LOOKAHEAD])
        row0 = pl.multiple_of(starts[j], WORD_ROWS)
        gbuf[pl.ds(j, WORD_ROWS, stride=GATHER_PITCH), :] = tab_ref[pl.ds(row0, WORD_ROWS), :]


class _StagedIndices:
    def __init__(self, idx_hbm, bufs, sems):
        self.hbm, self.bufs, self.sems = idx_hbm, bufs, sems
        self.half = bufs[0].shape[0]
        self.step = pl.program_id(0)
        self.last = pl.num_programs(0) - 1

        @pl.when(self.step == 0)
        def _():
            for k in range(2):
                self._copy(self.step, k).start()

    def _copy(self, step, k):
        rows = pl.ds((2 * step + k) * self.half, self.half)
        return pltpu.make_async_copy(self.hbm.at[rows], self.bufs[k], self.sems.at[k])

    def gather(self, t, tab_ref, gbuf):
        k, local = divmod(t, self.half)
        if local == 0:
            self._copy(self.step, k).wait()
        _gather_rows(self.bufs[k], local, tab_ref, gbuf)
        if local == self.half - 1:
            self._copy(jnp.minimum(self.step + 1, self.last), k).start()

    def finish(self):
        @pl.when(self.step == self.last)
        def _():
            for k in range(2):
                self._copy(self.last, k).wait()


def _gathered_matrix(gbuf):
    planes = [pltpu.bitcast(gbuf[i * GATHER_PITCH:i * GATHER_PITCH + PEER_SLOTS, :], BF16) for i in range(WORD_ROWS)]
    return jnp.concatenate(planes, axis=1)


def _lhs16(rows, width):
    r = lax.broadcasted_iota(jnp.int32, (16, width), 0)
    out = jnp.zeros((16, width), F32)
    for k, v in enumerate(rows):
        out = jnp.where(r == k, v, out)
    return out.astype(BF16)


PEER_TOKENS_PER_STEP = 128
PIPELINE_LAG = 2


def _pipelined_tokens(tb, gather, compute, bufs):
    n = len(bufs)
    for t in range(tb + PIPELINE_LAG):
        if t < tb:
            gather(t, bufs[t % n])
        if t >= PIPELINE_LAG:
            compute(t - PIPELINE_LAG, bufs[(t - PIPELINE_LAG) % n])


def _peer_u_kernel(idx_hbm, h_ref, tab_ref, g_ref, act_ref, hhi_ref, hlo_ref, c_ref, ga0, ga1, gb0, gb1,
                   idx_a, idx_b, idx_sems, *, tb):
    h = h_ref[...]
    hi = h.astype(BF16).astype(F32)
    hhi_ref[...] = hi
    hlo_ref[...] = h - hi
    even = (lax.broadcasted_iota(jnp.int32, (SUBLANES, 2 * PEER_SLOTS), 1) % 2) == 0
    indices = _StagedIndices(idx_hbm, (idx_a, idx_b), idx_sems)

    def compute(t, gbuf):
        row = pl.ds(t, 1)
        lhs = _lhs16([hhi_ref[row, :HALF], hhi_ref[row, HALF:], hlo_ref[row, :HALF], hlo_ref[row, HALF:]], HALF)
        res = _dot_nt(lhs, _gathered_matrix(gbuf))[:SUBLANES]
        s = res + pltpu.roll(res, 6, 0)
        c_ref[row, :] = jnp.where(even, s, pltpu.roll(s, 7, 0))[0:1]

    _pipelined_tokens(tb, lambda t, g: indices.gather(t, tab_ref, g), compute, (ga0, ga1, gb0, gb1))

    c = c_ref[...]
    a = c + pltpu.roll(c, 2 * PEER_SLOTS - 1, 1)
    gelu = 0.5 * a * (1.0 + lax.erf(a * np.float32(np.sqrt(0.5))))
    lane_even = (lax.broadcasted_iota(jnp.int32, a.shape, 1) % 2) == 0
    act_ref[...] = jnp.where(lane_even, gelu * g_ref[...], 0.0)
    indices.finish()


def _gather_scratch(tb):
    return ([pltpu.VMEM((WORD_ROWS * GATHER_PITCH, LANES), jnp.uint32)] * 4
            + [pltpu.SMEM((tb // 2, PEER_SLOTS), jnp.int32)] * 2 + [pltpu.SemaphoreType.DMA((2,))])


def _peer_u(idx, h2, tab, gates, tb=PEER_TOKENS_PER_STEP):
    T, D = h2.shape
    wide = pl.BlockSpec((tb, 2 * PEER_SLOTS), lambda i: (i, 0))
    return pl.pallas_call(
        functools.partial(_peer_u_kernel, tb=tb),
        grid=(T // tb,),
        in_specs=[pl.BlockSpec(memory_space=pl.ANY),
                  pl.BlockSpec((tb, D), lambda i: (i, 0)), _const_spec(tab.shape), wide],
        out_specs=wide,
        out_shape=jax.ShapeDtypeStruct((T, 2 * PEER_SLOTS), F32),
        scratch_shapes=[pltpu.VMEM((tb, D), F32)] * 2 + [pltpu.VMEM((tb, 2 * PEER_SLOTS), F32)] + _gather_scratch(tb),
        compiler_params=_cparams(1),
        name="peer_u",
    )(idx, h2, tab, gates)


def _peer_v_kernel(idx_hbm, act_ref, x_ref, tab_ref, o_ref, ahi_ref, alo_ref, bhi_ref, blo_ref, acc_ref,
                   ga0, ga1, gb0, gb1, idx_a, idx_b, idx_sems, *, tb):
    a = act_ref[...]
    b = pltpu.roll(a, 1, 1)
    for src, hi_ref, lo_ref in ((a, ahi_ref, alo_ref), (b, bhi_ref, blo_ref)):
        hi = src.astype(BF16).astype(F32)
        hi_ref[...] = hi
        lo_ref[...] = src - hi
    indices = _StagedIndices(idx_hbm, (idx_a, idx_b), idx_sems)

    def compute(t, gbuf):
        row = pl.ds(t, 1)
        lhs = _lhs16([ahi_ref[row, :], bhi_ref[row, :], alo_ref[row, :], blo_ref[row, :]], 2 * PEER_SLOTS)
        res = _dot(lhs, _gathered_matrix(gbuf))[:SUBLANES]
        s = res + pltpu.roll(res, 6, 0)
        acc_ref[row, :HALF] = s[0:1]
        acc_ref[row, HALF:] = s[1:2]

    _pipelined_tokens(tb, lambda t, g: indices.gather(t, tab_ref, g), compute, (ga0, ga1, gb0, gb1))
    o_ref[...] = x_ref[...] + acc_ref[...]
    indices.finish()


def _peer_v(idx, act, x2d, tab, tb=PEER_TOKENS_PER_STEP):
    T, D = x2d.shape
    tok = pl.BlockSpec((tb, D), lambda i: (i, 0))
    wide = pl.BlockSpec((tb, 2 * PEER_SLOTS), lambda i: (i, 0))
    return pl.pallas_call(
        functools.partial(_peer_v_kernel, tb=tb),
        grid=(T // tb,),
        in_specs=[pl.BlockSpec(memory_space=pl.ANY), wide, tok, _const_spec(tab.shape)],
        out_specs=tok,
        out_shape=jax.ShapeDtypeStruct((T, D), F32),
        scratch_shapes=[pltpu.VMEM((tb, 2 * PEER_SLOTS), F32)] * 4 + [pltpu.VMEM((tb, D), F32)] + _gather_scratch(tb),
        compiler_params=_cparams(1),
        name="peer_v",
    )(idx, act, x2d, tab)


def _ple_kernel(x_ref, p_ref, g_ref, wg_ref, wp_ref, o_ref):
    x = x_ref[...]
    h = _rms(x, g_ref[...]).astype(BF16)
    o_ref[...] = x + jax.nn.sigmoid(_dot(h, wg_ref[...])) * _dot(p_ref[...].astype(BF16), wp_ref[...])


def _ple(x2d, p2d, gain, wg, wp, tm=512):
    T, D = x2d.shape
    row = lambda w: pl.BlockSpec((tm, w), lambda i: (i, 0))
    return pl.pallas_call(
        _ple_kernel,
        grid=(T // tm,),
        in_specs=[row(D), row(p2d.shape[1]), _const_spec((1, D)), _const_spec(wg.shape), _const_spec(wp.shape)],
        out_specs=row(D),
        out_shape=jax.ShapeDtypeStruct((T, D), F32),
        compiler_params=_cparams(1),
        name="ple",
    )(x2d, p2d, gain, wg, wp)


def _rotary_tables(seq):
    half = ROT_DIM // 2
    inv_freq = jnp.power(jnp.float32(ROPE_THETA), -jnp.arange(half, dtype=F32) * 2.0 / ROT_DIM)
    ang = jnp.arange(seq).astype(F32)[:, None] * inv_freq[None, :]
    cos, sin = jnp.cos(ang), jnp.sin(ang)
    pad = HEAD_DIM - ROT_DIM
    ra = jnp.concatenate([cos, cos, jnp.ones((seq, pad), F32)], axis=1)
    rb = jnp.concatenate([-sin, sin, jnp.zeros((seq, pad), F32)], axis=1)
    return jnp.tile(ra, (1, DIL_HEADS_PER_GROUP)), jnp.tile(rb, (1, DIL_HEADS_PER_GROUP))


def kernel(x, p, norm_mix, w_in, qk_norm_na, na_rel_bias, qk_norm_dil, w_branch_na, w_branch_dil, w_out, norm_ffn,
           peer_w_query, peer_sub_keys, peer_expert_u, peer_expert_v, norm_ple, w_ple_gate, w_ple):
    B, S, D = x.shape
    T = B * S
    depth = w_in.shape[0]
    wna = NA_HEADS * HEAD_DIM
    wdil = len(DIL_CONFIGS) * DIL_GROUP_WIDTH
    rot_a, rot_b = _rotary_tables(S)
    x2d = x.reshape(T, D)
    for i in range(depth):
        wi = w_in[i].astype(BF16)
        nna = jnp.tile(qk_norm_na[i], (1, NA_HEADS))
        ndil = jnp.tile(qk_norm_dil[i], (1, DIL_HEADS_PER_GROUP))
        (qa, ka, va, q1, k1, v1, q2, k2, v2, q3, k3, v3, sgn, sgd) = _in_proj(
            x2d, norm_mix[i][None], wi[:, :3 * wna], wi[:, 3 * wna:3 * wna + 3 * wdil], wi[:, 3 * wna + 3 * wdil:],
            nna, ndil, rot_a, rot_b, S)

        ona = _na_attn(qa, ka, va, _na_bias_table(na_rel_bias[i], S // GRID_W), B, S)
        dil = [_dil_attn(q, k, v, B, S, window, dilation)
               for (q, k, v), (window, dilation) in zip(((q1, k1, v1), (q2, k2, v2), (q3, k3, v3)), DIL_CONFIGS)]

        x1, h2 = _merge(x2d, ona, [o for o, _ in dil], [l for _, l in dil], sgn, sgd,
                        w_branch_na[i].astype(BF16), w_branch_dil[i].astype(BF16), w_out[i].astype(BF16),
                        norm_ffn[i][None])

        idx, gates = _peer_topk(h2, peer_w_query[i].astype(BF16), peer_sub_keys[i, 0].astype(BF16),
                                peer_sub_keys[i, 1].astype(BF16))
        act = _peer_u(idx, h2, _pack_table(peer_expert_u[i]), gates)
        x2 = _peer_v(idx, act, x1, _pack_table(peer_expert_v[i]))

        x2d = _ple(x2, p[i].reshape(T, -1), norm_ple[i][None], w_ple_gate[i].astype(BF16), w_ple[i].astype(BF16))
    return x2d.reshape(B, S, D)
```
